```python
import math
import jax
import jax.numpy as jnp
from jax import lax
import numpy as np

D_MODEL = 1024
BATCH = 8
SEQ = 4096
DEPTH = 2

DEEPNORM_ALPHA = (2 * DEPTH) ** 0.25
DEEPNORM_BETA = (8 * DEPTH) ** -0.25
NORM_EPS = 1e-5

N_BRANCH = 4
BRANCH_WIDTH = D_MODEL // 2

SSD_HEAD_DIM = 64
SSD_HEADS = BRANCH_WIDTH // SSD_HEAD_DIM
SSD_GROUPS = 2
SSD_STATE = 64
SSD_CONV = 4
SSD_CHUNK = 128
SSD_CONV_DIM = BRANCH_WIDTH + 2 * SSD_GROUPS * SSD_STATE

DIL_HEAD_DIM = 64
DIL_HEADS = BRANCH_WIDTH // DIL_HEAD_DIM
DIL_PATTERNS = ((128, 1), (512, 4), (2048, 16))
DIL_GROUPS = len(DIL_PATTERNS)
DIL_BLOCK = 128

RET_HEADS = 4
RET_DK = BRANCH_WIDTH // RET_HEADS
RET_DV = BRANCH_WIDTH // RET_HEADS
RET_CHUNK = 128

HGRN_HEADS = 4
HGRN_DK = BRANCH_WIDTH // HGRN_HEADS
HGRN_DV = BRANCH_WIDTH // HGRN_HEADS
HGRN_CHUNK = 64

D_FF = 256 * ((8 * D_MODEL // 3 + 255) // 256)
FFN_CONV = 3

SECTION_SIZES = (
    BRANCH_WIDTH, SSD_CONV_DIM, SSD_HEADS,
    DIL_GROUPS * BRANCH_WIDTH, BRANCH_WIDTH, BRANCH_WIDTH,
    BRANCH_WIDTH, BRANCH_WIDTH, BRANCH_WIDTH, BRANCH_WIDTH,
    BRANCH_WIDTH, BRANCH_WIDTH, BRANCH_WIDTH, BRANCH_WIDTH,
)
N_IN = sum(SECTION_SIZES)

kernel_name = 'hybrid_ssd_dilated_retention_hgrn2_block'


def _layer_norm(x, g, b):
    xf = x.astype(jnp.float32)
    xc = xf - jnp.mean(xf, axis=-1, keepdims=True)
    y = xc * lax.rsqrt(jnp.mean(xc * xc, axis=-1, keepdims=True) + NORM_EPS)
    return (y * g + b).astype(x.dtype)


def _rms(x):
    return x * lax.rsqrt(jnp.mean(x * x, axis=-1, keepdims=True) + NORM_EPS)


def _causal_dwconv(x, w, bias):
    width, ch = w.shape
    y = lax.conv_general_dilated(x, w.astype(x.dtype)[:, None, :], window_strides=(1,),
                                 padding=((width - 1, 0),), dimension_numbers=('NWC', 'WIO', 'NWC'),
                                 feature_group_count=ch)
    return y + bias.astype(x.dtype)


def _alibi_slopes(n):
    def pow2(k):
        start = 2.0 ** (-8.0 / k)
        return [start ** (i + 1) for i in range(k)]
    if math.log2(n).is_integer():
        s = pow2(n)
    else:
        c = 2 ** math.floor(math.log2(n))
        s = pow2(c) + pow2(2 * c)[0::2][: n - c]
    return jnp.asarray(s, jnp.float32)


def _segsum(a):
    t = a.shape[-1]
    cs = jnp.cumsum(a, axis=-1)
    mask = jnp.tril(jnp.ones((t, t), bool))
    return jnp.where(mask, cs[..., :, None] - cs[..., None, :], -jnp.inf)


def _ssd_chunked(x, a, bm, cm):
    b, L, nh, hp = x.shape
    nc, e = L // SSD_CHUNK, nh // SSD_GROUPS
    x = x.reshape(b, nc, SSD_CHUNK, SSD_GROUPS, e, hp)
    bm = bm.reshape(b, nc, SSD_CHUNK, SSD_GROUPS, SSD_STATE)
    cm = cm.reshape(b, nc, SSD_CHUNK, SSD_GROUPS, SSD_STATE)
    a = a.reshape(b, nc, SSD_CHUNK, SSD_GROUPS, e).transpose(0, 3, 4, 1, 2)
    a_cs = jnp.cumsum(a, axis=-1)
    cb = jnp.einsum('bclgn,bcsgn->bgcls', cm, bm)
    y_diag = jnp.einsum('bgecls,bcsgep->bclgep', cb[:, :, None] * jnp.exp(_segsum(a)), x)
    decay_states = jnp.exp(a_cs[..., -1:] - a_cs)
    states = jnp.einsum('bclgn,bgecl,bclgep->cbgepn', bm, decay_states, x)
    chunk_decay = jnp.moveaxis(jnp.exp(a_cs[..., -1]), -1, 0)

    def step(s, inp):
        dec, st = inp
        return s * dec[..., None, None] + st, s

    _, s_prev = lax.scan(step, jnp.zeros(states.shape[1:], jnp.float32), (chunk_decay, states))
    y_off = jnp.einsum('bclgn,cbgepn,bgecl->bclgep', cm, s_prev, jnp.exp(a_cs))
    return (y_diag + y_off).reshape(b, L, nh, hp)


def _ssd_mixer(z, xbc, dt, conv_w, conv_b, dt_bias, a_log, d_skip, norm_w):
    b, L, _ = z.shape
    f32 = jnp.float32
    xbc = jax.nn.silu(_causal_dwconv(xbc, conv_w, conv_b))
    gn = SSD_GROUPS * SSD_STATE
    xs = xbc[..., :BRANCH_WIDTH].astype(f32).reshape(b, L, SSD_HEADS, SSD_HEAD_DIM)
    bm = xbc[..., BRANCH_WIDTH:BRANCH_WIDTH + gn].astype(f32).reshape(b, L, SSD_GROUPS, SSD_STATE)
    cm = xbc[..., BRANCH_WIDTH + gn:].astype(f32).reshape(b, L, SSD_GROUPS, SSD_STATE)
    dt = jax.nn.softplus(dt.astype(f32) + dt_bias.astype(f32))
    a = -jnp.exp(a_log.astype(f32))
    y = _ssd_chunked(xs * dt[..., None], a * dt, bm, cm) + d_skip.astype(f32)[:, None] * xs
    y = y.reshape(b, L, BRANCH_WIDTH) * jax.nn.silu(z.astype(f32))
    y = _rms(y.reshape(b, L, SSD_GROUPS, BRANCH_WIDTH // SSD_GROUPS)).reshape(b, L, BRANCH_WIDTH)
    return (y * norm_w.astype(f32)).astype(z.dtype)


def _banded_causal_attention(q, k, v, slopes, dilation, n_back):
    nseq, n, nh, dh = q.shape
    nb = -(-n // DIL_BLOCK)
    padw = ((0, 0), (0, nb * DIL_BLOCK - n), (0, 0), (0, 0))
    qb = jnp.pad(q, padw).reshape(nseq, nb, DIL_BLOCK, nh, dh)
    kb = jnp.pad(k, padw).reshape(nseq, nb, DIL_BLOCK, nh, dh)
    vb = jnp.pad(v, padw).reshape(nseq, nb, DIL_BLOCK, nh, dh)
    shift = ((0, 0), (1, 0), (0, 0), (0, 0), (0, 0))
    kk = jnp.concatenate([jnp.pad(kb, shift)[:, :-1], kb], axis=2)
    vv = jnp.concatenate([jnp.pad(vb, shift)[:, :-1], vb], axis=2)
    s = jnp.einsum('nbqhd,nbkhd->nbhqk', qb, kk) * (dh ** -0.5)
    dist = jnp.arange(DIL_BLOCK)[:, None] - jnp.arange(2 * DIL_BLOCK)[None, :] + DIL_BLOCK
    key_pos = jnp.arange(nb)[:, None] * DIL_BLOCK + jnp.arange(2 * DIL_BLOCK)[None, :] - DIL_BLOCK
    valid = ((dist >= 0) & (dist <= n_back))[None] & (key_pos >= 0)[:, None, :]
    s = s - slopes[:, None, None] * (dilation * dist).astype(jnp.float32)
    s = jnp.where(valid[:, None], s, -jnp.inf)
    m = jnp.max(s, axis=-1, keepdims=True)
    p = jnp.exp(s - m)
    l = jnp.sum(p, axis=-1)
    o = jnp.einsum('nbhqk,nbkhd->nbqhd', p, vv) / jnp.swapaxes(l, 2, 3)[..., None]
    lse = jnp.swapaxes(m[..., 0] + jnp.log(l), 2, 3)
    o = o.reshape(nseq, nb * DIL_BLOCK, nh, dh)[:, :n]
    lse = lse.reshape(nseq, nb * DIL_BLOCK, nh)[:, :n]
    return o, lse


def _to_strided(t, r):
    b, L = t.shape[:2]
    t = t.reshape((b, L // r, r) + t.shape[2:])
    return jnp.moveaxis(t, 2, 1).reshape((b * r, L // r) + t.shape[3:])


def _from_strided(t, b, r):
    t = t.reshape((b, r) + t.shape[1:])
    t = jnp.moveaxis(t, 1, 2)
    return t.reshape((b, t.shape[1] * r) + t.shape[3:])


def _dilated_attention(q, k, v):
    b, L = q.shape[:2]
    f32 = jnp.float32
    q, k, v = q.astype(f32), k.astype(f32), v.astype(f32)
    slopes = _alibi_slopes(DIL_GROUPS * DIL_HEADS).reshape(DIL_GROUPS, DIL_HEADS)
    outs, lses = [], []
    for g, (window, r) in enumerate(DIL_PATTERNS):
        o, lse = _banded_causal_attention(_to_strided(q[:, :, g], r), _to_strided(k, r),
                                          _to_strided(v, r), slopes[g], r, window // r)
        outs.append(_from_strided(o, b, r))
        lses.append(_from_strided(lse, b, r))
    weights = jax.nn.softmax(jnp.stack(lses), axis=0)
    o = jnp.einsum('gblh,gblhd->blhd', weights, jnp.stack(outs))
    return o.reshape(b, L, DIL_HEADS * DIL_HEAD_DIM)


def _retention(q, k, v, g):
    b, L, _ = q.shape
    nc, C = L // RET_CHUNK, RET_CHUNK
    f32 = jnp.float32
    q = q.astype(f32).reshape(b, nc, C, RET_HEADS, RET_DK)
    k = k.astype(f32).reshape(b, nc, C, RET_HEADS, RET_DK) * (RET_DK ** -0.5)
    v = v.astype(f32).reshape(b, nc, C, RET_HEADS, RET_DV)
    log_gamma = jnp.log(1.0 - 2.0 ** (-5.0 - jnp.arange(RET_HEADS, dtype=f32)))
    pos = jnp.arange(C, dtype=f32)
    rel = pos[:, None] - pos[None, :]
    decay = jnp.where(rel >= 0, jnp.exp(log_gamma[:, None, None] * jnp.maximum(rel, 0.0)), 0.0)
    inner = jnp.einsum('bnhij,bnjhv->bnihv', jnp.einsum('bnihk,bnjhk->bnhij', q, k) * decay, v)
    k_dec = k * jnp.exp(log_gamma[None, :] * (C - 1.0 - pos)[:, None])[..., None]
    chunk_states = jnp.einsum('bnjhk,bnjhv->nbhkv', k_dec, v)
    chunk_decay = jnp.exp(log_gamma * C)[:, None, None]

    def step(s, st):
        return s * chunk_decay + st, s

    _, s_prev = lax.scan(step, jnp.zeros(chunk_states.shape[1:], f32), chunk_states)
    q_dec = q * jnp.exp(log_gamma[None, :] * (pos + 1.0)[:, None])[..., None]
    o = (inner + jnp.einsum('bnihk,nbhkv->bnihv', q_dec, s_prev)).reshape(b, L, RET_HEADS, RET_DV)
    oc = o - jnp.mean(o, axis=-1, keepdims=True)
    o = oc * lax.rsqrt(jnp.mean(oc * oc, axis=-1, keepdims=True) + NORM_EPS)
    return (o.reshape(b, L, RET_HEADS * RET_DV) * jax.nn.silu(g.astype(f32))).astype(g.dtype)


def _hgrn2(q, f, i, g, lb, norm_w):
    b, L, _ = q.shape
    nc = L // HGRN_CHUNK
    f32 = jnp.float32

    def chunks(t, d):
        return t.reshape(b, nc, HGRN_CHUNK, HGRN_HEADS, d).transpose(1, 0, 3, 2, 4)

    forget = lb + (1.0 - lb) * jax.nn.sigmoid(f.astype(f32))
    qc = chunks(jax.nn.silu(q.astype(f32)), HGRN_DK)
    lfc = chunks(jnp.log(forget), HGRN_DK)
    kc = chunks(1.0 - forget, HGRN_DK)
    vc = chunks(i.astype(f32), HGRN_DV)
    causal = jnp.tril(jnp.ones((HGRN_CHUNK, HGRN_CHUNK), bool))[:, :, None]

    def step(s, inp):
        qt, lf, kt, vt = inp
        lam = jnp.cumsum(lf, axis=2)
        diff = lam[:, :, :, None, :] - lam[:, :, None, :, :]
        w = jnp.exp(jnp.where(causal, diff, -jnp.inf))
        attn = jnp.einsum('bhtk,bhjk,bhtjk->bhtj', qt, kt, w)
        o = jnp.einsum('bhtj,bhjv->bhtv', attn, vt) + jnp.einsum('bhtk,bhkv->bhtv', qt * jnp.exp(lam), s)
        lam_last = lam[:, :, -1:, :]
        s = s * jnp.exp(lam_last[:, :, 0, :, None]) + jnp.einsum('bhjk,bhjv->bhkv', kt * jnp.exp(lam_last - lam), vt)
        return s, o

    s0 = jnp.zeros((b, HGRN_HEADS, HGRN_DK, HGRN_DV), f32)
    _, o = lax.scan(step, s0, (qc, lfc, kc, vc))
    o = o.transpose(1, 0, 3, 2, 4).reshape(b, L, HGRN_HEADS, HGRN_DV)
    o = _rms(o) * norm_w.astype(f32)
    return (o.reshape(b, L, HGRN_HEADS * HGRN_DV) * jax.nn.silu(g.astype(f32))).astype(q.dtype)


def _hybrid_mixer(h, w_in, w_merge, ssd_conv_w, ssd_conv_b, ssd_dt_bias, ssd_a_log, ssd_d, ssd_norm_w,
                  lb, hgrn_norm_w, w_branch_out, w_o):
    b, L, _ = h.shape
    cuts = np.cumsum(SECTION_SIZES)[:-1].tolist()
    (z, xbc, dt, dq, dk, dv, rq, rk, rv, rg, hq, hf, hi, hg) = jnp.split(h @ w_in, cuts, axis=-1)
    o_ssd = _ssd_mixer(z, xbc, dt, ssd_conv_w, ssd_conv_b, ssd_dt_bias, ssd_a_log, ssd_d, ssd_norm_w)
    o_dil = _dilated_attention(dq.reshape(b, L, DIL_GROUPS, DIL_HEADS, DIL_HEAD_DIM),
                               dk.reshape(b, L, DIL_HEADS, DIL_HEAD_DIM),
                               dv.reshape(b, L, DIL_HEADS, DIL_HEAD_DIM)).astype(h.dtype)
    o_ret = _retention(rq, rk, rv, rg)
    o_hgrn = _hgrn2(hq, hf, hi, hg, lb, hgrn_norm_w)
    gates = jax.nn.sigmoid((h @ w_merge).astype(jnp.float32)).astype(h.dtype).reshape(b, L, N_BRANCH, D_MODEL)
    y = (gates[:, :, 0] * (o_ssd @ w_branch_out[0]) + gates[:, :, 1] * (o_dil @ w_branch_out[1])
         + gates[:, :, 2] * (o_ret @ w_branch_out[2]) + gates[:, :, 3] * (o_hgrn @ w_branch_out[3]))
    return y @ w_o


def _conv_ffn(h, w_up, conv_w, conv_b, w_down):
    u = _causal_dwconv(h @ w_up, conv_w, conv_b)
    gate, val = jnp.split(u, 2, axis=-1)
    return (jax.nn.silu(gate) * val) @ w_down


def setup_inputs(seed: int = 0) -> dict:
    key = jax.random.key(seed)
    ks = jax.random.split(key, 24)
    f32 = jnp.float32

    def nrm(k, shape, scale):
        return scale * jax.random.normal(k, shape, f32)

    dt = jnp.exp(jax.random.uniform(ks[8], (DEPTH, SSD_HEADS), f32, math.log(1e-3), math.log(1e-1)))
    return {
        'x': nrm(ks[0], (BATCH, SEQ, D_MODEL), 1.0),
        'c': nrm(ks[1], (BATCH, D_MODEL), 1.0),
        'w_ada': nrm(ks[2], (DEPTH, D_MODEL, 6 * D_MODEL), 0.5 * D_MODEL ** -0.5),
        'b_ada': nrm(ks[3], (DEPTH, 6 * D_MODEL), 0.01),
        'w_in': nrm(ks[4], (DEPTH, D_MODEL, N_IN), D_MODEL ** -0.5),
        'w_merge': nrm(ks[5], (DEPTH, D_MODEL, N_BRANCH * D_MODEL), D_MODEL ** -0.5),
        'ssd_conv_w': nrm(ks[6], (DEPTH, SSD_CONV, SSD_CONV_DIM), SSD_CONV ** -0.5),
        'ssd_conv_b': nrm(ks[7], (DEPTH, SSD_CONV_DIM), 0.01),
        'ssd_dt_bias': dt + jnp.log(-jnp.expm1(-dt)),
        'ssd_a_log': jnp.log(jax.random.uniform(ks[9], (DEPTH, SSD_HEADS), f32, 1.0, 16.0)),
        'ssd_d': 1.0 + nrm(ks[10], (DEPTH, SSD_HEADS), 0.1),
        'ssd_norm_w': 1.0 + nrm(ks[11], (DEPTH, BRANCH_WIDTH), 0.02),
        'hgrn_lb': nrm(ks[12], (DEPTH, HGRN_HEADS * HGRN_DK), 0.1),
        'hgrn_norm_w': 1.0 + nrm(ks[13], (DEPTH, HGRN_DV), 0.02),
        'w_branch_out': nrm(ks[14], (DEPTH, N_BRANCH, BRANCH_WIDTH, D_MODEL), BRANCH_WIDTH ** -0.5),
        'w_o': nrm(ks[15], (DEPTH, D_MODEL, D_MODEL), DEEPNORM_BETA * D_MODEL ** -0.5),
        'ln1_g': 1.0 + nrm(ks[16], (DEPTH, D_MODEL), 0.02),
        'ln1_b': nrm(ks[17], (DEPTH, D_MODEL), 0.01),
        'w_up': nrm(ks[18], (DEPTH, D_MODEL, 2 * D_FF), D_MODEL ** -0.5),
        'ffn_conv_w': nrm(ks[19], (DEPTH, FFN_CONV, 2 * D_FF), FFN_CONV ** -0.5),
        'ffn_conv_b': nrm(ks[20], (DEPTH, 2 * D_FF), 0.01),
        'w_down': nrm(ks[21], (DEPTH, D_FF, D_MODEL), DEEPNORM_BETA * D_FF ** -0.5),
        'ln2_g': 1.0 + nrm(ks[22], (DEPTH, D_MODEL), 0.02),
        'ln2_b': nrm(ks[23], (DEPTH, D_MODEL), 0.01),
    }


def reference(x, c, w_ada, b_ada, w_in, w_merge, ssd_conv_w, ssd_conv_b, ssd_dt_bias, ssd_a_log, ssd_d,
              ssd_norm_w, hgrn_lb, hgrn_norm_w, w_branch_out, w_o, ln1_g, ln1_b, w_up, ffn_conv_w,
              ffn_conv_b, w_down, ln2_g, ln2_b):
    lbs = jax.nn.softmax(hgrn_lb.astype(jnp.float32), axis=0)
    lbs = jnp.cumsum(lbs, axis=0) - lbs[0]
    cond = jax.nn.silu(c)
    for layer in range(DEPTH):
        mod = cond @ w_ada[layer] + b_ada[layer]
        sh1, sc1, g1, sh2, sc2, g2 = (m[:, None, :] for m in jnp.split(mod, 6, axis=-1))
        h = x * (1.0 + sc1) + sh1
        mix = _hybrid_mixer(h, w_in[layer], w_merge[layer], ssd_conv_w[layer], ssd_conv_b[layer],
                            ssd_dt_bias[layer], ssd_a_log[layer], ssd_d[layer], ssd_norm_w[layer],
                            lbs[layer], hgrn_norm_w[layer], w_branch_out[layer], w_o[layer])
        x = _layer_norm(DEEPNORM_ALPHA * x + g1 * mix, ln1_g[layer], ln1_b[layer])
        h = x * (1.0 + sc2) + sh2
        ffn = _conv_ffn(h, w_up[layer], ffn_conv_w[layer], ffn_conv_b[layer], w_down[layer])
        x = _layer_norm(DEEPNORM_ALPHA * x + g2 * ffn, ln2_g[layer], ln2_b[layer])
    return x
```

```python
import functools
import math

import numpy as np
import jax
import jax.numpy as jnp
from jax import lax
from jax.experimental import pallas as pl
from jax.experimental.pallas import tpu as pltpu

F32 = jnp.float32
BF16 = jnp.bfloat16

D_MODEL = 1024
DEPTH = 2
DEEPNORM_ALPHA = (2 * DEPTH) ** 0.25
NORM_EPS = 1e-5

N_BRANCH = 4
BRANCH_WIDTH = D_MODEL // 2

SSD_HEAD_DIM = 64
SSD_HEADS = BRANCH_WIDTH // SSD_HEAD_DIM
SSD_GROUPS = 2
SSD_STATE = 64
SSD_CONV = 4
SSD_CONV_DIM = BRANCH_WIDTH + 2 * SSD_GROUPS * SSD_STATE

DIL_HEAD_DIM = 64
DIL_HEADS = BRANCH_WIDTH // DIL_HEAD_DIM
DIL_PATTERNS = ((128, 1), (512, 4), (2048, 16))
DIL_GROUPS = len(DIL_PATTERNS)
DIL_BLOCK = 128

RET_HEADS = 4
RET_DK = BRANCH_WIDTH // RET_HEADS

HGRN_HEADS = 4
HGRN_DK = BRANCH_WIDTH // HGRN_HEADS
HGRN_SUB = 16

D_FF = 256 * ((8 * D_MODEL // 3 + 255) // 256)
FFN_CONV = 3
FF_CHUNK = 256
N_FF_CHUNKS = D_FF // FF_CHUNK

SECTION_SIZES = (
    BRANCH_WIDTH, SSD_CONV_DIM, SSD_HEADS,
    DIL_GROUPS * BRANCH_WIDTH, BRANCH_WIDTH, BRANCH_WIDTH,
    BRANCH_WIDTH, BRANCH_WIDTH, BRANCH_WIDTH, BRANCH_WIDTH,
    BRANCH_WIDTH, BRANCH_WIDTH, BRANCH_WIDTH, BRANCH_WIDTH,
)
SECTION_NAMES = ("z", "xbc", "dt", "dq", "dk", "dv", "rq", "rk", "rv", "rg", "hq", "hf", "hi", "hg")

LANES = 128
CHUNK = 128
PA_WIDTH = 13 * BRANCH_WIDTH
PB_WIDTH = 3 * BRANCH_WIDTH
VMEM_LIMIT = 56 * 1024 * 1024


def _cparams(*sem):
    return pltpu.CompilerParams(dimension_semantics=sem, vmem_limit_bytes=VMEM_LIMIT)


def _silu(v):
    return v * jax.nn.sigmoid(v)


def _mm(a, b):
    return jnp.dot(a, b, preferred_element_type=F32)


def _mm_nt(a, b):
    return lax.dot_general(a, b, (((1,), (1,)), ((), ())), preferred_element_type=F32)


def _mm_tn(a, b):
    return lax.dot_general(a, b, (((0,), (0,)), ((), ())), preferred_element_type=F32)


def _split2(a):
    hi = a.astype(BF16)
    lo = (a - hi.astype(F32)).astype(BF16)
    return hi, lo


def _split3(a):
    hi = a.astype(BF16)
    r = a - hi.astype(F32)
    mid = r.astype(BF16)
    lo = (r - mid.astype(F32)).astype(BF16)
    return hi, mid, lo


def _mm_f32(a, b):
    ah, al = _split2(a)
    bh, bl = _split2(b)
    return _mm(ah, bh) + (_mm(ah, bl) + _mm(al, bh))


def _cumsum_rows(tri, a):
    hi, mid, lo = _split3(a)
    return _mm(tri, hi) + (_mm(tri, mid) + _mm(tri, lo))


def _expand(a, e):
    hi, lo = _split2(a)
    return _mm(hi, e) + _mm(lo, e)


def _iota(shape, dim):
    return lax.broadcasted_iota(jnp.int32, shape, dim)


def _tri(n):
    return jnp.where(_iota((n, n), 1) <= _iota((n, n), 0), 1.0, 0.0).astype(BF16)


def _head_expander(width):
    shape = (LANES, BRANCH_WIDTH)
    shift = int(math.log2(width))
    return jnp.where(jnp.right_shift(_iota(shape, 1), shift) == _iota(shape, 0), 1.0, 0.0).astype(BF16)


def _layer_norm_rows(v, g, b):
    vc = v - jnp.mean(v, axis=-1, keepdims=True)
    return vc * lax.rsqrt(jnp.mean(vc * vc, axis=-1, keepdims=True) + NORM_EPS) * g + b


def _mod_kernel(c_ref, w_ref, b_ref, o_ref):
    o_ref[0] = _mm_f32(_silu(c_ref[...]), w_ref[0]) + b_ref[0]


def _modulation(c, w_ada, b_ada):
    nb = c.shape[0]
    tn = 1536
    return pl.pallas_call(
        _mod_kernel,
        grid=(DEPTH, 6 * D_MODEL // tn),
        in_specs=[pl.BlockSpec((nb, D_MODEL), lambda l, j: (0, 0)),
                  pl.BlockSpec((1, D_MODEL, tn), lambda l, j: (l, 0, j)),
                  pl.BlockSpec((1, 1, tn), lambda l, j: (l, 0, j))],
        out_specs=pl.BlockSpec((1, nb, tn), lambda l, j: (l, 0, j)),
        out_shape=jax.ShapeDtypeStruct((DEPTH, nb, 6 * D_MODEL), F32),
        compiler_params=_cparams("parallel", "parallel"),
        name="adaln_mod",
    )(c, w_ada, b_ada.reshape(DEPTH, 1, 6 * D_MODEL))


def _proj_kernel(x_ref, mod_ref, w_ref, o_ref, h_ref):
    @pl.when(pl.program_id(1) == 0)
    def _():
        m = mod_ref[...]
        h_ref[...] = (x_ref[...] * (1.0 + m[1:2]) + m[0:1]).astype(BF16)

    o_ref[...] = _mm(h_ref[...], w_ref[...]).astype(o_ref.dtype)


def _project(x2, mod, w, seq, tn, out_dtype, name):
    t = x2.shape[0]
    tm = 512
    per_b = seq // tm
    n = w.shape[1]
    return pl.pallas_call(
        _proj_kernel,
        grid=(t // tm, n // tn),
        in_specs=[pl.BlockSpec((tm, D_MODEL), lambda i, j: (i, 0)),
                  pl.BlockSpec((None, 6, D_MODEL), lambda i, j: (i // per_b, 0, 0)),
                  pl.BlockSpec((D_MODEL, tn), lambda i, j: (0, j))],
        out_specs=pl.BlockSpec((tm, tn), lambda i, j: (i, j)),
        out_shape=jax.ShapeDtypeStruct((t, n), out_dtype),
        scratch_shapes=[pltpu.VMEM((tm, D_MODEL), BF16)],
        compiler_params=_cparams("parallel", "arbitrary"),
        name=name,
    )(x2, mod, w)


def _ssd_kernel(z_ref, xbc_ref, dt_ref, cw_ref, cb_ref, dtb_ref, alog_ref, dsk_ref, nw_ref, o_ref, xbuf, st):
    c = CHUNK
    halo = 8

    @pl.when(pl.program_id(1) == 0)
    def _():
        xbuf[0:halo, :] = jnp.zeros((halo, SSD_CONV_DIM), F32)
        st[...] = jnp.zeros_like(st)

    xbuf[halo:halo + c, :] = xbc_ref[...]
    acc = jnp.broadcast_to(cb_ref[...], (c, SSD_CONV_DIM))
    for k in range(SSD_CONV):
        off = halo - (SSD_CONV - 1) + k
        acc = acc + cw_ref[k:k + 1, :] * xbuf[off:off + c, :]
    xbuf[0:halo, :] = xbuf[c:c + halo, :]
    y = _silu(acc)
    xs = y[:, :BRANCH_WIDTH]
    bm = y[:, BRANCH_WIDTH:BRANCH_WIDTH + LANES]
    cm = y[:, BRANCH_WIDTH + LANES:]

    dtr = dt_ref[...] + dtb_ref[...]
    dt = jnp.maximum(dtr, 0.0) + jnp.log1p(jnp.exp(-jnp.abs(dtr)))
    da = dt * (-jnp.exp(alog_ref[...]))
    cs = _cumsum_rows(_tri(c), da)
    cs_t = cs.T
    tot = cs[c - 1:c, :]

    stack = jnp.concatenate([dt, jnp.exp(tot - cs), jnp.exp(cs), jnp.broadcast_to(jnp.exp(tot), (8, LANES))], axis=0)
    ex = _expand(stack, _head_expander(SSD_HEAD_DIM))
    dt_e, ds_e, ecs_e, dec_e = ex[0:c], ex[c:2 * c], ex[2 * c:3 * c], ex[3 * c:3 * c + 1]

    xdt = xs * dt_e
    xds = xdt * ds_e
    causal = _iota((c, c), 1) <= _iota((c, c), 0)
    lane = _iota((c, LANES), 1)
    bm16 = bm.astype(BF16)
    cbs = []
    for g in range(SSD_GROUPS):
        cm_g = jnp.where(jnp.right_shift(lane, 6) == g, cm, 0.0).astype(BF16)
        cbs.append(_mm_nt(cm_g, bm16))
    parts = []
    for p in range(SSD_HEADS // 2):
        g = (2 * p) // (SSD_HEADS // SSD_GROUPS)
        ms = []
        for e in range(2):
            h = 2 * p + e
            diff = cs[:, h:h + 1] - cs_t[h:h + 1, :]
            ms.append((cbs[g] * jnp.exp(jnp.where(causal, diff, -jnp.inf))).astype(BF16))
        xp = xdt[:, p * LANES:(p + 1) * LANES]
        xbd = jnp.concatenate([jnp.where(lane < SSD_HEAD_DIM, xp, 0.0), jnp.where(lane >= SSD_HEAD_DIM, xp, 0.0)],
                              axis=0).astype(BF16)
        parts.append(_mm(jnp.concatenate(ms, axis=1), xbd))
    y_diag = jnp.concatenate(parts, axis=1)

    s_prev = st[...]
    y_off = _mm(cm.astype(BF16), s_prev.astype(BF16)) * ecs_e
    upd = _mm(bm.T.astype(BF16), xds.astype(BF16))
    shape = (LANES, BRANCH_WIDTH)
    same_group = jnp.right_shift(_iota(shape, 0), 6) == jnp.right_shift(_iota(shape, 1), 8)
    st[...] = s_prev * dec_e + jnp.where(same_group, upd, 0.0)

    yv = (y_diag + y_off + xs * dsk_ref[...]) * _silu(z_ref[...].astype(F32))
    gw = BRANCH_WIDTH // SSD_GROUPS
    outs = []
    for g in range(SSD_GROUPS):
        yg = yv[:, g * gw:(g + 1) * gw]
        outs.append(yg * lax.rsqrt(jnp.mean(yg * yg, axis=-1, keepdims=True) + NORM_EPS))
    o_ref[...] = (jnp.concatenate(outs, axis=1) * nw_ref[...]).astype(o_ref.dtype)


def _ssd(pa, pb, conv_w, conv_b, dt_bias, a_log, d_skip, norm_w):
    nb, seq, _ = pa.shape
    pad = LANES - SSD_HEADS
    row = lambda v: v.reshape(1, -1).astype(F32)
    args = (conv_w.astype(F32), row(conv_b), row(jnp.pad(dt_bias, (0, pad))), row(jnp.pad(a_log, (0, pad))),
            row(jnp.repeat(d_skip, SSD_HEAD_DIM)), row(norm_w))
    full = lambda a: pl.BlockSpec(a.shape, lambda b, c: (0, 0))
    return pl.pallas_call(
        _ssd_kernel,
        grid=(nb, seq // CHUNK),
        in_specs=[pl.BlockSpec((None, CHUNK, BRANCH_WIDTH), lambda b, c: (b, c, 12)),
                  pl.BlockSpec((None, CHUNK, SSD_CONV_DIM), lambda b, c: (b, c, 1)),
                  pl.BlockSpec((None, CHUNK, LANES), lambda b, c: (b, c, 4))] + [full(a) for a in args],
        out_specs=pl.BlockSpec((None, CHUNK, BRANCH_WIDTH), lambda b, c: (b, c, 0)),
        out_shape=jax.ShapeDtypeStruct((nb, seq, BRANCH_WIDTH), BF16),
        scratch_shapes=[pltpu.VMEM((CHUNK + 8, SSD_CONV_DIM), F32), pltpu.VMEM((LANES, BRANCH_WIDTH), F32)],
        compiler_params=_cparams("parallel", "arbitrary"),
        name="ssd_mixer",
    )(pa, pb, pb, *args)


def _alibi_slopes(n):
    def pow2(k):
        start = 2.0 ** (-8.0 / k)
        return [start ** (i + 1) for i in range(k)]
    if math.log2(n).is_integer():
        s = pow2(n)
    else:
        c = 2 ** math.floor(math.log2(n))
        s = pow2(c) + pow2(2 * c)[0::2][: n - c]
    return [float(np.float32(v)) for v in s]


def _dil_kernel(dilation, n_back, slopes, q_ref, kc_ref, kp_ref, vc_ref, vp_ref, o_ref, lse_ref):
    blk = DIL_BLOCK
    qi = _iota((blk, 2 * blk), 0)
    kj = _iota((blk, 2 * blk), 1)
    dist = qi - kj + blk
    first_key = jnp.where(pl.program_id(2) > 0, 0, blk)
    valid = (dist >= 0) & (dist <= n_back) & (kj >= first_key)
    distf = (dist * dilation).astype(F32)
    lane = _iota((blk, LANES), 1)
    lse_tile = jnp.zeros((blk, LANES), F32)
    outs = []
    for p in range(DIL_HEADS // 2):
        sl = slice(p * LANES, (p + 1) * LANES)
        qp = q_ref[:, sl].astype(F32)
        kk = jnp.concatenate([kp_ref[:, sl], kc_ref[:, sl]], axis=0)
        vv = jnp.concatenate([vp_ref[:, sl], vc_ref[:, sl]], axis=0)
        pair = []
        for e in range(2):
            h = 2 * p + e
            own = (lane < DIL_HEAD_DIM) if e == 0 else (lane >= DIL_HEAD_DIM)
            qm = jnp.where(own, qp, 0.0).astype(BF16)
            s = _mm_nt(qm, kk) * (DIL_HEAD_DIM ** -0.5) - slopes[h] * distf
            s = jnp.where(valid, s, -jnp.inf)
            m = jnp.max(s, axis=-1, keepdims=True)
            pexp = jnp.exp(s - m)
            l = jnp.sum(pexp, axis=-1, keepdims=True)
            pair.append(_mm(pexp.astype(BF16), vv) / l)
            lse_tile = jnp.where(lane == h, m + jnp.log(l), lse_tile)
        outs.append(jnp.where(lane < DIL_HEAD_DIM, pair[0], pair[1]))
    o_ref[...] = jnp.concatenate(outs, axis=1)
    lse_ref[...] = lse_tile


def _dilated_group(pa, g):
    nb, seq, _ = pa.shape
    window, r = DIL_PATTERNS[g]
    n = seq // r
    blocks = PA_WIDTH // BRANCH_WIDTH
    view = pa.reshape(nb, n, r * PA_WIDTH)
    slopes = _alibi_slopes(DIL_GROUPS * DIL_HEADS)[g * DIL_HEADS:(g + 1) * DIL_HEADS]
    blockspec = lambda col, prev: pl.BlockSpec(
        (None, DIL_BLOCK, BRANCH_WIDTH),
        (lambda b, rho, i: (b, jnp.maximum(i - 1, 0), rho * blocks + col)) if prev
        else (lambda b, rho, i: (b, i, rho * blocks + col)))
    o, lse = pl.pallas_call(
        functools.partial(_dil_kernel, r, window // r, slopes),
        grid=(nb, r, n // DIL_BLOCK),
        in_specs=[blockspec(g, False), blockspec(3, False), blockspec(3, True),
                  blockspec(4, False), blockspec(4, True)],
        out_specs=[pl.BlockSpec((None, DIL_BLOCK, BRANCH_WIDTH), lambda b, rho, i: (b, i, rho)),
                   pl.BlockSpec((None, DIL_BLOCK, LANES), lambda b, rho, i: (b, i, rho))],
        out_shape=[jax.ShapeDtypeStruct((nb, n, r * BRANCH_WIDTH), F32),
                   jax.ShapeDtypeStruct((nb, n, r * LANES), F32)],
        compiler_params=_cparams("parallel", "parallel", "arbitrary"),
        name=f"dilated_attn_g{g}",
    )(view, view, view, view, view)
    return o.reshape(nb * seq, BRANCH_WIDTH), lse.reshape(nb * seq, LANES)


def _dil_combine_kernel(l0_ref, l1_ref, l2_ref, o0_ref, o1_ref, o2_ref, out_ref):
    ls = [l0_ref[...], l1_ref[...], l2_ref[...]]
    m = jnp.maximum(jnp.maximum(ls[0], ls[1]), ls[2])
    es = [jnp.exp(l - m) for l in ls]
    den = es[0] + es[1] + es[2]
    e = _head_expander(DIL_HEAD_DIM)
    acc = None
    for ev, o_ref in zip(es, (o0_ref, o1_ref, o2_ref)):
        term = _expand(ev / den, e) * o_ref[...]
        acc = term if acc is None else acc + term
    out_ref[...] = acc.astype(out_ref.dtype)


def _dilated(pa):
    nb, seq, _ = pa.shape
    res = [_dilated_group(pa, g) for g in range(DIL_GROUPS)]
    t = nb * seq
    tm = 512
    lspec = pl.BlockSpec((tm, LANES), lambda i: (i, 0))
    ospec = pl.BlockSpec((tm, BRANCH_WIDTH), lambda i: (i, 0))
    return pl.pallas_call(
        _dil_combine_kernel,
        grid=(t // tm,),
        in_specs=[lspec] * 3 + [ospec] * 3,
        out_specs=ospec,
        out_shape=jax.ShapeDtypeStruct((t, BRANCH_WIDTH), BF16),
        compiler_params=_cparams("parallel"),
        name="dilated_combine",
    )(res[0][1], res[1][1], res[2][1], res[0][0], res[1][0], res[2][0])


def _ret_kernel(q_ref, k_ref, v_ref, g_ref, o_ref, st):
    c = CHUNK
    scale = RET_DK ** -0.5

    @pl.when(pl.program_id(1) == 0)
    def _():
        st[...] = jnp.zeros_like(st)

    row = _iota((c, c), 0)
    col = _iota((c, c), 1)
    rel = (row - col).astype(F32)
    pos = row.astype(F32)
    outs = []
    for h in range(RET_HEADS):
        sl = slice(h * RET_DK, (h + 1) * RET_DK)
        lg = math.log(1.0 - 2.0 ** (-5.0 - h))
        decay = jnp.where(row >= col, jnp.exp(lg * jnp.maximum(rel, 0.0)), 0.0) * scale
        qh, kh, vh = q_ref[:, sl], k_ref[:, sl], v_ref[:, sl]
        inner = _mm((_mm_nt(qh, kh) * decay).astype(BF16), vh)
        s_prev = st[h]
        q_dec = (qh.astype(F32) * jnp.exp(lg * (pos + 1.0))).astype(BF16)
        o = inner + _mm(q_dec, s_prev.astype(BF16))
        k_dec = (kh.astype(F32) * (jnp.exp(lg * (c - 1.0 - pos)) * scale)).astype(BF16)
        st[h] = s_prev * math.exp(lg * c) + _mm_tn(k_dec, vh)
        oc = o - jnp.mean(o, axis=-1, keepdims=True)
        outs.append(oc * lax.rsqrt(jnp.mean(oc * oc, axis=-1, keepdims=True) + NORM_EPS))
    o_ref[...] = (jnp.concatenate(outs, axis=1) * _silu(g_ref[...].astype(F32))).astype(o_ref.dtype)


def _retention(pa):
    nb, seq, _ = pa.shape
    spec = lambda col: pl.BlockSpec((None, CHUNK, BRANCH_WIDTH), lambda b, c: (b, c, col))
    return pl.pallas_call(
        _ret_kernel,
        grid=(nb, seq // CHUNK),
        in_specs=[spec(5), spec(6), spec(7), spec(8)],
        out_specs=spec(0),
        out_shape=jax.ShapeDtypeStruct((nb, seq, BRANCH_WIDTH), BF16),
        scratch_shapes=[pltpu.VMEM((RET_HEADS, RET_DK, RET_DK), F32)],
        compiler_params=_cparams("parallel", "arbitrary"),
        name="retention",
    )(pa, pa, pa, pa)


def _hgrn_kernel(layer, q_ref, f_ref, i_ref, g_ref, lb_ref, nw_ref, o_ref, st):
    c = CHUNK
    width = BRANCH_WIDTH
    neg_inf = -jnp.inf

    @pl.when(pl.program_id(1) == 0)
    def _():
        st[...] = jnp.zeros_like(st)

    rows = [lb_ref[l:l + 1, :] for l in range(DEPTH)]
    mx = functools.reduce(jnp.maximum, rows)
    es = [jnp.exp(r - mx) for r in rows]
    den = functools.reduce(lambda a, b: a + b, es)
    sm = [e / den for e in es]
    lb = functools.reduce(lambda a, b: a + b, sm[:layer + 1]) - sm[0]

    forget = lb + (1.0 - lb) * jax.nn.sigmoid(f_ref[...])
    lf = jnp.log(forget)
    kk = 1.0 - forget
    q = _silu(q_ref[...].astype(F32))
    v16 = i_ref[...]
    vf = v16.astype(F32)
    lam = _cumsum_rows(_tri(c), lf)

    rowi = _iota((c, width), 0)
    sub = jnp.bitwise_and(rowi, HGRN_SUB - 1)
    heads = [slice(h * HGRN_DK, (h + 1) * HGRN_DK) for h in range(HGRN_HEADS)]

    o_acc = [jnp.zeros((c, HGRN_DK), F32) for _ in heads]
    for d in range(HGRN_SUB):
        if d == 0:
            prod = q * kk
            vd = vf
        else:
            ld = pltpu.roll(lam, d, 0)
            prod = q * pltpu.roll(kk, d, 0) * jnp.exp(jnp.where(sub >= d, lam - ld, neg_inf))
            vd = pltpu.roll(vf, d, 0)
        for h, sl in enumerate(heads):
            o_acc[h] = o_acc[h] + jnp.sum(prod[:, sl], axis=-1, keepdims=True) * vd[:, sl]

    ti = _iota((c, c), 0)
    tj = _iota((c, c), 1)
    attn = [jnp.zeros((c, c), F32) for _ in heads]
    size = c
    while size > HGRN_SUB:
        half = size // 2
        ref = jnp.broadcast_to(lam[half - 1:half, :], (c, width))
        for b in range(1, c // size):
            ref = jnp.where(rowi >= b * size, lam[b * size + half - 1:b * size + half, :], ref)
        upper = jnp.bitwise_and(rowi, size - 1) >= half
        qs = (q * jnp.exp(jnp.where(upper, lam - ref, neg_inf))).astype(BF16)
        ks = (kk * jnp.exp(jnp.where(upper, neg_inf, ref - lam))).astype(BF16)
        shift = int(math.log2(size))
        same = jnp.right_shift(ti, shift) == jnp.right_shift(tj, shift)
        for h, sl in enumerate(heads):
            attn[h] = attn[h] + jnp.where(same, _mm_nt(qs[:, sl], ks[:, sl]), 0.0)
        size = half

    lam_last = lam[c - 1:c, :]
    q_in = (q * jnp.exp(lam)).astype(BF16)
    k_out = (kk * jnp.exp(lam_last - lam)).astype(BF16)
    e_last = jnp.exp(lam_last)
    outs = []
    for h, sl in enumerate(heads):
        s_prev = st[h]
        o = o_acc[h] + _mm(attn[h].astype(BF16), v16[:, sl]) + _mm_nt(q_in[:, sl], s_prev.astype(BF16))
        st[h] = s_prev * e_last[:, sl] + _mm_tn(v16[:, sl], k_out[:, sl])
        outs.append(o * lax.rsqrt(jnp.mean(o * o, axis=-1, keepdims=True) + NORM_EPS) * nw_ref[...])
    o_ref[...] = (jnp.concatenate(outs, axis=1) * _silu(g_ref[...].astype(F32))).astype(o_ref.dtype)


def _hgrn(pa, pb, layer, hgrn_lb, norm_w):
    nb, seq, _ = pa.shape
    spec = lambda col: pl.BlockSpec((None, CHUNK, BRANCH_WIDTH), lambda b, c: (b, c, col))
    lb = hgrn_lb.astype(F32)
    nw = norm_w.reshape(1, HGRN_DK).astype(F32)
    return pl.pallas_call(
        functools.partial(_hgrn_kernel, layer),
        grid=(nb, seq // CHUNK),
        in_specs=[spec(9), spec(0), spec(10), spec(11),
                  pl.BlockSpec(lb.shape, lambda b, c: (0, 0)), pl.BlockSpec(nw.shape, lambda b, c: (0, 0))],
        out_specs=spec(0),
        out_shape=jax.ShapeDtypeStruct((nb, seq, BRANCH_WIDTH), BF16),
        scratch_shapes=[pltpu.VMEM((HGRN_HEADS, HGRN_DK, HGRN_DK), F32)],
        compiler_params=_cparams("parallel", "arbitrary"),
        name="hgrn2",
    )(pa, pb, pa, pa, lb, nw)


def _merge_kernel(x_ref, mod_ref, o0_ref, o1_ref, o2_ref, o3_ref, wm_ref, wb_ref, wo_ref, g_ref, b_ref, out_ref):
    m = mod_ref[...]
    x = x_ref[...]
    h = (x * (1.0 + m[1:2]) + m[0:1]).astype(BF16)
    y = None
    for i, o_ref in enumerate((o0_ref, o1_ref, o2_ref, o3_ref)):
        gate = jax.nn.sigmoid(_mm(h, wm_ref[:, i * D_MODEL:(i + 1) * D_MODEL]))
        term = gate * _mm(o_ref[...], wb_ref[i])
        y = term if y is None else y + term
    mix = _mm(y.astype(BF16), wo_ref[...])
    out_ref[...] = _layer_norm_rows(DEEPNORM_ALPHA * x + m[2:3] * mix, g_ref[...], b_ref[...])


def _const_spec(shape):
    zeros = (0,) * len(shape)
    return pl.BlockSpec(shape, lambda *_: zeros, pipeline_mode=pl.Buffered(1))


def _merge(x2, mod, branches, w_merge, w_branch, w_o, ln_g, ln_b, seq):
    t = x2.shape[0]
    tm = 256
    per_b = seq // tm
    rows = pl.BlockSpec((tm, D_MODEL), lambda i: (i, 0))
    brow = pl.BlockSpec((tm, BRANCH_WIDTH), lambda i: (i, 0))
    return pl.pallas_call(
        _merge_kernel,
        grid=(t // tm,),
        in_specs=[rows, pl.BlockSpec((None, 6, D_MODEL), lambda i: (i // per_b, 0, 0))] + [brow] * N_BRANCH
        + [_const_spec(w_merge.shape), _const_spec(w_branch.shape), _const_spec(w_o.shape),
           _const_spec((1, D_MODEL)), _const_spec((1, D_MODEL))],
        out_specs=rows,
        out_shape=jax.ShapeDtypeStruct((t, D_MODEL), F32),
        compiler_params=_cparams("parallel"),
        name="merge_out_ln",
    )(x2, mod, *branches, w_merge, w_branch, w_o, ln_g.reshape(1, -1), ln_b.reshape(1, -1))


def _ffn_kernel(x_ref, mod_ref, wu_ref, cw_ref, cb_ref, wd_ref, g_ref, b_ref, out_ref, ubuf, tail, acc):
    tm = x_ref.shape[0]
    halo = 8

    @pl.when(pl.program_id(1) == 0)
    def _():
        tail[...] = jnp.zeros_like(tail)

    m = mod_ref[...]
    x = x_ref[...]
    h = (x * (1.0 + m[4:5]) + m[3:4]).astype(BF16)
    acc[...] = jnp.zeros_like(acc)

    def body(j, carry):
        ubuf[0:halo, :] = tail[j]
        ubuf[halo:halo + tm, :] = _mm(h, wu_ref[j])
        tail[j] = ubuf[tm:tm + halo, :]
        cw = cw_ref[j]
        u = jnp.broadcast_to(cb_ref[j], (tm, 2 * FF_CHUNK))
        for k in range(FFN_CONV):
            off = halo - (FFN_CONV - 1) + k
            u = u + cw[k:k + 1, :] * ubuf[off:off + tm, :]
        act = (_silu(u[:, :FF_CHUNK]) * u[:, FF_CHUNK:]).astype(BF16)
        acc[...] += _mm(act, wd_ref[j])
        return carry

    lax.fori_loop(0, N_FF_CHUNKS, body, 0)
    out_ref[...] = _layer_norm_rows(DEEPNORM_ALPHA * x + m[5:6] * acc[...], g_ref[...], b_ref[...])


def _ffn(x2, mod, w_up, conv_w, conv_b, w_down, ln_g, ln_b, nb, seq):
    tm = 256
    per_b = seq // tm
    rows = pl.BlockSpec((tm, D_MODEL), lambda b, i: (b * per_b + i, 0))
    return pl.pallas_call(
        _ffn_kernel,
        grid=(nb, per_b),
        in_specs=[rows, pl.BlockSpec((None, 6, D_MODEL), lambda b, i: (b, 0, 0)),
                  _const_spec(w_up.shape), _const_spec(conv_w.shape), _const_spec(conv_b.shape),
                  _const_spec(w_down.shape), _const_spec((1, D_MODEL)), _const_spec((1, D_MODEL))],
        out_specs=rows,
        out_shape=jax.ShapeDtypeStruct((nb * seq, D_MODEL), F32),
        scratch_shapes=[pltpu.VMEM((tm + 8, 2 * FF_CHUNK), F32),
                        pltpu.VMEM((N_FF_CHUNKS, 8, 2 * FF_CHUNK), F32),
                        pltpu.VMEM((tm, D_MODEL), F32)],
        compiler_params=_cparams("parallel", "arbitrary"),
        name="conv_ffn_ln",
    )(x2, mod, w_up, conv_w, conv_b, w_down, ln_g.reshape(1, -1), ln_b.reshape(1, -1))


def _pack_in_proj(w_in):
    offs = np.concatenate([[0], np.cumsum(SECTION_SIZES)])
    sec = {n: w_in[:, offs[i]:offs[i + 1]] for i, n in enumerate(SECTION_NAMES)}
    wa = jnp.concatenate([sec[n] for n in ("dq", "dk", "dv", "rq", "rk", "rv", "rg", "hq", "hi", "hg", "z")], axis=1)
    dt_pad = jnp.zeros((D_MODEL, 2 * LANES - SSD_HEADS), w_in.dtype)
    wb = jnp.concatenate([sec["hf"], sec["dt"], dt_pad, sec["xbc"]], axis=1)
    return wa.astype(BF16), wb.astype(BF16)


def _chunk_gate_value(a):
    lead = a.shape[:-1]
    a = a.reshape(lead + (2, N_FF_CHUNKS, FF_CHUNK))
    a = jnp.moveaxis(a, -2, 0)
    return a.reshape((N_FF_CHUNKS,) + lead + (2 * FF_CHUNK,))


def kernel(x, c, w_ada, b_ada, w_in, w_merge, ssd_conv_w, ssd_conv_b, ssd_dt_bias, ssd_a_log, ssd_d, ssd_norm_w, hgrn_lb, hgrn_norm_w, w_branch_out, w_o, ln1_g, ln1_b, w_up, ffn_conv_w, ffn_conv_b, w_down, ln2_g, ln2_b):
    nb, seq, _ = x.shape
    assert seq % (DIL_PATTERNS[-1][1] * DIL_BLOCK) == 0 and x.shape[-1] == D_MODEL
    mods = _modulation(c, w_ada, b_ada).reshape(DEPTH, nb, 6, D_MODEL)
    x2 = x.reshape(nb * seq, D_MODEL)
    for layer in range(DEPTH):
        mod = mods[layer]
        wa, wb = _pack_in_proj(w_in[layer])
        pa = _project(x2, mod, wa, seq, PA_WIDTH // 2, BF16, "in_proj_bf16").reshape(nb, seq, PA_WIDTH)
        pb = _project(x2, mod, wb, seq, PB_WIDTH, F32, "in_proj_f32").reshape(nb, seq, PB_WIDTH)
        o_ssd = _ssd(pa, pb, ssd_conv_w[layer], ssd_conv_b[layer], ssd_dt_bias[layer], ssd_a_log[layer],
                     ssd_d[layer], ssd_norm_w[layer])
        o_dil = _dilated(pa)
        o_ret = _retention(pa)
        o_hgrn = _hgrn(pa, pb, layer, hgrn_lb, hgrn_norm_w[layer])
        flat = lambda a: a.reshape(nb * seq, BRANCH_WIDTH)
        x2 = _merge(x2, mod, (flat(o_ssd), o_dil, flat(o_ret), flat(o_hgrn)), w_merge[layer].astype(BF16),
                    w_branch_out[layer].astype(BF16), w_o[layer].astype(BF16), ln1_g[layer], ln1_b[layer], seq)
        x2 = _ffn(x2, mod, _chunk_gate_value(w_up[layer]).astype(BF16), _chunk_gate_value(ffn_conv_w[layer]),
                  _chunk_gate_value(ffn_conv_b[layer].reshape(1, -1)), w_down[layer].reshape(N_FF_CHUNKS, FF_CHUNK, D_MODEL).astype(BF16),
                  ln2_g[layer], ln2_b[layer], nb, seq)
    return x2.reshape(nb, seq, D_MODEL)
```

```python
import functools
import math

import numpy as np
import jax
import jax.numpy as jnp
from jax import lax
from jax.experimental import pallas as pl
from jax.experimental.pallas import tpu as pltpu

F32 = jnp.float32
BF16 = jnp.bfloat16

D_MODEL = 1024
DEPTH = 2
DEEPNORM_ALPHA = (2 * DEPTH) ** 0.25
NORM_EPS = 1e-5

N_BRANCH = 4
BRANCH_WIDTH = D_MODEL // 2

SSD_HEAD_DIM = 64
SSD_HEADS = BRANCH_WIDTH // SSD_HEAD_DIM
SSD_GROUPS = 2
SSD_STATE = 64
SSD_CONV = 4
SSD_CONV_DIM = BRANCH_WIDTH + 2 * SSD_GROUPS * SSD_STATE

DIL_HEAD_DIM = 64
DIL_HEADS = BRANCH_WIDTH // DIL_HEAD_DIM
DIL_PATTERNS = ((128, 1), (512, 4), (2048, 16))
DIL_GROUPS = len(DIL_PATTERNS)
DIL_BLOCK = 128

RET_HEADS = 4
RET_DK = BRANCH_WIDTH // RET_HEADS

HGRN_HEADS = 4
HGRN_DK = BRANCH_WIDTH // HGRN_HEADS
HGRN_SUB = 16

D_FF = 256 * ((8 * D_MODEL // 3 + 255) // 256)
FFN_CONV = 3
FF_CHUNK = 256
N_FF_CHUNKS = D_FF // FF_CHUNK

SECTION_SIZES = (
    BRANCH_WIDTH, SSD_CONV_DIM, SSD_HEADS,
    DIL_GROUPS * BRANCH_WIDTH, BRANCH_WIDTH, BRANCH_WIDTH,
    BRANCH_WIDTH, BRANCH_WIDTH, BRANCH_WIDTH, BRANCH_WIDTH,
    BRANCH_WIDTH, BRANCH_WIDTH, BRANCH_WIDTH, BRANCH_WIDTH,
)
SECTION_NAMES = ("z", "xbc", "dt", "dq", "dk", "dv", "rq", "rk", "rv", "rg", "hq", "hf", "hi", "hg")

LANES = 128
CHUNK = 128
PA_COLS = {n: i for i, n in enumerate(("rq", "rk", "rv", "rg", "hq", "hi", "hg", "z"))}
PD_COLS = {n: i for i, n in enumerate(("q0", "k", "v", "q1", "q2"))}
PB_HF_COL, PB_DT_COL, PB_XBC_COL = 0, 4, 1
VMEM_LIMIT = 56 * 1024 * 1024


def _cparams(*sem):
    return pltpu.CompilerParams(dimension_semantics=sem, vmem_limit_bytes=VMEM_LIMIT)


def _silu(v):
    return v * jax.nn.sigmoid(v)


def _mm(a, b):
    return jnp.dot(a, b, preferred_element_type=F32)


def _mm_nt(a, b):
    return lax.dot_general(a, b, (((1,), (1,)), ((), ())), preferred_element_type=F32)


def _mm_tn(a, b):
    return lax.dot_general(a, b, (((0,), (0,)), ((), ())), preferred_element_type=F32)


def _split2(a):
    hi = a.astype(BF16)
    lo = (a - hi.astype(F32)).astype(BF16)
    return hi, lo


def _split3(a):
    hi = a.astype(BF16)
    r = a - hi.astype(F32)
    mid = r.astype(BF16)
    lo = (r - mid.astype(F32)).astype(BF16)
    return hi, mid, lo


def _mm_f32(a, b):
    ah, al = _split2(a)
    bh, bl = _split2(b)
    return _mm(ah, bh) + (_mm(ah, bl) + _mm(al, bh))


def _cumsum_rows(tri, a):
    hi, mid, lo = _split3(a)
    return _mm(tri, hi) + (_mm(tri, mid) + _mm(tri, lo))


def _expand(a, e):
    hi, lo = _split2(a)
    return _mm(hi, e) + _mm(lo, e)


def _iota(shape, dim):
    return lax.broadcasted_iota(jnp.int32, shape, dim)


def _tri(n):
    return jnp.where(_iota((n, n), 1) <= _iota((n, n), 0), 1.0, 0.0).astype(BF16)


def _head_expander(width):
    shape = (LANES, BRANCH_WIDTH)
    shift = int(math.log2(width))
    return jnp.where(jnp.right_shift(_iota(shape, 1), shift) == _iota(shape, 0), 1.0, 0.0).astype(BF16)


def _layer_norm_rows(v, g, b):
    vc = v - jnp.mean(v, axis=-1, keepdims=True)
    return vc * lax.rsqrt(jnp.mean(vc * vc, axis=-1, keepdims=True) + NORM_EPS) * g + b


def _mod_kernel(c_ref, w_ref, b_ref, o_ref):
    o_ref[0] = _mm_f32(_silu(c_ref[...]), w_ref[0]) + b_ref[0]


def _modulation(c, w_ada, b_ada):
    nb = c.shape[0]
    tn = 1536
    return pl.pallas_call(
        _mod_kernel,
        grid=(DEPTH, 6 * D_MODEL // tn),
        in_specs=[pl.BlockSpec((nb, D_MODEL), lambda l, j: (0, 0)),
                  pl.BlockSpec((1, D_MODEL, tn), lambda l, j: (l, 0, j)),
                  pl.BlockSpec((1, 1, tn), lambda l, j: (l, 0, j))],
        out_specs=pl.BlockSpec((1, nb, tn), lambda l, j: (l, 0, j)),
        out_shape=jax.ShapeDtypeStruct((DEPTH, nb, 6 * D_MODEL), F32),
        compiler_params=_cparams("parallel", "parallel"),
        name="adaln_mod",
    )(c, w_ada, b_ada.reshape(DEPTH, 1, 6 * D_MODEL))


def _const_spec(shape):
    zeros = (0,) * len(shape)
    return pl.BlockSpec(shape, lambda *_: zeros, pipeline_mode=pl.Buffered(1))


def _proj_kernel(x_ref, mod_ref, wa_ref, wd_ref, wb_ref, oa_ref, od_ref, ob_ref):
    m = mod_ref[...]
    h = (x_ref[...] * (1.0 + m[1:2]) + m[0:1]).astype(BF16)
    oa_ref[...] = _mm(h, wa_ref[...]).astype(oa_ref.dtype)
    od_ref[...] = _mm(h, wd_ref[...]).astype(od_ref.dtype)
    ob_ref[...] = _mm(h, wb_ref[...]).astype(ob_ref.dtype)


def _project(x2, mod, wa, wd, wb, seq):
    t = x2.shape[0]
    tm = 256
    per_b = seq // tm
    outs = [(wa.shape[1], BF16), (wd.shape[1], BF16), (wb.shape[1], F32)]
    return pl.pallas_call(
        _proj_kernel,
        grid=(t // tm,),
        in_specs=[pl.BlockSpec((tm, D_MODEL), lambda i: (i, 0)),
                  pl.BlockSpec((None, 6, D_MODEL), lambda i: (i // per_b, 0, 0)),
                  _const_spec(wa.shape), _const_spec(wd.shape), _const_spec(wb.shape)],
        out_specs=[pl.BlockSpec((tm, n), lambda i: (i, 0)) for n, _ in outs],
        out_shape=[jax.ShapeDtypeStruct((t, n), dt) for n, dt in outs],
        compiler_params=_cparams("parallel"),
        name="in_proj",
    )(x2, mod, wa, wd, wb)


def _ssd_kernel(z_ref, xbc_ref, dt_ref, cw_ref, cb_ref, dtb_ref, alog_ref, dsk_ref, nw_ref, o_ref, xbuf, st):
    c = CHUNK
    halo = 8

    @pl.when(pl.program_id(1) == 0)
    def _():
        xbuf[0:halo, :] = jnp.zeros((halo, SSD_CONV_DIM), F32)
        st[...] = jnp.zeros_like(st)

    xbuf[halo:halo + c, :] = xbc_ref[...]
    acc = jnp.broadcast_to(cb_ref[...], (c, SSD_CONV_DIM))
    for k in range(SSD_CONV):
        off = halo - (SSD_CONV - 1) + k
        acc = acc + cw_ref[k:k + 1, :] * xbuf[off:off + c, :]
    xbuf[0:halo, :] = xbuf[c:c + halo, :]
    y = _silu(acc)
    xs = y[:, :BRANCH_WIDTH]
    bm = y[:, BRANCH_WIDTH:BRANCH_WIDTH + LANES]
    cm = y[:, BRANCH_WIDTH + LANES:]

    dtr = dt_ref[...] + dtb_ref[...]
    dt = jnp.maximum(dtr, 0.0) + jnp.log1p(jnp.exp(-jnp.abs(dtr)))
    da = dt * (-jnp.exp(alog_ref[...]))
    cs = _cumsum_rows(_tri(c), da)
    cs_t = cs.T
    tot = cs[c - 1:c, :]

    stack = jnp.concatenate([dt, jnp.exp(tot - cs), jnp.exp(cs), jnp.broadcast_to(jnp.exp(tot), (8, LANES))], axis=0)
    ex = _expand(stack, _head_expander(SSD_HEAD_DIM))
    dt_e, ds_e, ecs_e, dec_e = ex[0:c], ex[c:2 * c], ex[2 * c:3 * c], ex[3 * c:3 * c + 1]

    xdt = xs * dt_e
    xds = xdt * ds_e
    causal = _iota((c, c), 1) <= _iota((c, c), 0)
    lane = _iota((c, LANES), 1)
    bm16 = bm.astype(BF16)
    cbs = []
    for g in range(SSD_GROUPS):
        cm_g = jnp.where(jnp.right_shift(lane, 6) == g, cm, 0.0).astype(BF16)
        cbs.append(_mm_nt(cm_g, bm16))
    parts = []
    for p in range(SSD_HEADS // 2):
        g = (2 * p) // (SSD_HEADS // SSD_GROUPS)
        ms = []
        for e in range(2):
            h = 2 * p + e
            diff = cs[:, h:h + 1] - cs_t[h:h + 1, :]
            ms.append((cbs[g] * jnp.exp(jnp.where(causal, diff, -jnp.inf))).astype(BF16))
        xp = xdt[:, p * LANES:(p + 1) * LANES]
        xbd = jnp.concatenate([jnp.where(lane < SSD_HEAD_DIM, xp, 0.0), jnp.where(lane >= SSD_HEAD_DIM, xp, 0.0)],
                              axis=0).astype(BF16)
        parts.append(_mm(jnp.concatenate(ms, axis=1), xbd))
    y_diag = jnp.concatenate(parts, axis=1)

    s_prev = st[...]
    y_off = _mm(cm.astype(BF16), s_prev.astype(BF16)) * ecs_e
    upd = _mm(bm.T.astype(BF16), xds.astype(BF16))
    shape = (LANES, BRANCH_WIDTH)
    same_group = jnp.right_shift(_iota(shape, 0), 6) == jnp.right_shift(_iota(shape, 1), 8)
    st[...] = s_prev * dec_e + jnp.where(same_group, upd, 0.0)

    yv = (y_diag + y_off + xs * dsk_ref[...]) * _silu(z_ref[...].astype(F32))
    gw = BRANCH_WIDTH // SSD_GROUPS
    outs = []
    for g in range(SSD_GROUPS):
        yg = yv[:, g * gw:(g + 1) * gw]
        outs.append(yg * lax.rsqrt(jnp.mean(yg * yg, axis=-1, keepdims=True) + NORM_EPS))
    o_ref[...] = (jnp.concatenate(outs, axis=1) * nw_ref[...]).astype(o_ref.dtype)


def _ssd(pa, pb, conv_w, conv_b, dt_bias, a_log, d_skip, norm_w):
    nb, seq, _ = pa.shape
    pad = LANES - SSD_HEADS
    row = lambda v: v.reshape(1, -1).astype(F32)
    args = (conv_w.astype(F32), row(conv_b), row(jnp.pad(dt_bias, (0, pad))), row(jnp.pad(a_log, (0, pad))),
            row(jnp.repeat(d_skip, SSD_HEAD_DIM)), row(norm_w))
    full = lambda a: pl.BlockSpec(a.shape, lambda b, c: (0, 0))
    return pl.pallas_call(
        _ssd_kernel,
        grid=(nb, seq // CHUNK),
        in_specs=[pl.BlockSpec((None, CHUNK, BRANCH_WIDTH), lambda b, c: (b, c, PA_COLS["z"])),
                  pl.BlockSpec((None, CHUNK, SSD_CONV_DIM), lambda b, c: (b, c, PB_XBC_COL)),
                  pl.BlockSpec((None, CHUNK, LANES), lambda b, c: (b, c, PB_DT_COL))] + [full(a) for a in args],
        out_specs=pl.BlockSpec((None, CHUNK, BRANCH_WIDTH), lambda b, c: (b, c, 0)),
        out_shape=jax.ShapeDtypeStruct((nb, seq, BRANCH_WIDTH), BF16),
        scratch_shapes=[pltpu.VMEM((CHUNK + 8, SSD_CONV_DIM), F32), pltpu.VMEM((LANES, BRANCH_WIDTH), F32)],
        compiler_params=_cparams("parallel", "arbitrary"),
        name="ssd_mixer",
    )(pa, pb, pb, *args)


def _alibi_slopes(n):
    def pow2(k):
        start = 2.0 ** (-8.0 / k)
        return [start ** (i + 1) for i in range(k)]
    if math.log2(n).is_integer():
        s = pow2(n)
    else:
        c = 2 ** math.floor(math.log2(n))
        s = pow2(c) + pow2(2 * c)[0::2][: n - c]
    return [float(np.float32(v)) for v in s]


def _dil_kernel(dilation, n_back, slopes, q_ref, kc_ref, kp_ref, vc_ref, vp_ref, o_ref, lse_ref):
    blk = DIL_BLOCK
    rho = pl.program_id(2)
    rows = pl.ds(rho, blk, stride=dilation) if dilation > 1 else slice(None)
    qi = _iota((blk, 2 * blk), 0)
    kj = _iota((blk, 2 * blk), 1)
    dist = qi - kj + blk
    first_key = jnp.where(pl.program_id(1) > 0, 0, blk)
    valid = (dist >= 0) & (dist <= n_back) & (kj >= first_key)
    distf = (dist * dilation).astype(F32)
    lane = _iota((blk, LANES), 1)
    lse_tile = jnp.zeros((blk, LANES), F32)
    for p in range(DIL_HEADS // 2):
        sl = slice(p * LANES, (p + 1) * LANES)
        qp = q_ref[:, sl].astype(F32)
        kk = jnp.concatenate([kp_ref[:, sl], kc_ref[:, sl]], axis=0)
        vv = jnp.concatenate([vp_ref[:, sl], vc_ref[:, sl]], axis=0)
        pair = []
        for e in range(2):
            h = 2 * p + e
            own = (lane < DIL_HEAD_DIM) if e == 0 else (lane >= DIL_HEAD_DIM)
            qm = jnp.where(own, qp, 0.0).astype(BF16)
            s = _mm_nt(qm, kk) * (DIL_HEAD_DIM ** -0.5) - slopes[h] * distf
            s = jnp.where(valid, s, -jnp.inf)
            m = jnp.max(s, axis=-1, keepdims=True)
            pexp = jnp.exp(s - m)
            l = jnp.sum(pexp, axis=-1, keepdims=True)
            pair.append(_mm(pexp.astype(BF16), vv) / l)
            lse_tile = jnp.where(lane == h, m + jnp.log(l), lse_tile)
        o_ref[p, rows, :] = jnp.where(lane < DIL_HEAD_DIM, pair[0], pair[1])
    lse_ref[rows, :] = lse_tile


def _dilated_group(pd, g):
    nb, seq, _ = pd.shape
    window, r = DIL_PATTERNS[g]
    n = seq // r
    if r == 1:
        view, cols, blocks = pd, [PD_COLS[c] for c in ("q0", "k", "v")], len(PD_COLS)
    else:
        w = BRANCH_WIDTH
        pick = [pd[:, :, PD_COLS[c] * w:(PD_COLS[c] + 1) * w] for c in (f"q{g}", "k", "v")]
        view, cols, blocks = jnp.concatenate(pick, axis=-1).reshape(nb, n, r * 3 * w), [0, 1, 2], 3
    slopes = _alibi_slopes(DIL_GROUPS * DIL_HEADS)[g * DIL_HEADS:(g + 1) * DIL_HEADS]
    blockspec = lambda col, prev: pl.BlockSpec(
        (None, DIL_BLOCK, BRANCH_WIDTH),
        (lambda b, i, rho: (b, jnp.maximum(i - 1, 0), rho * blocks + col)) if prev
        else (lambda b, i, rho: (b, i, rho * blocks + col)))
    pairs = DIL_HEADS // 2
    return pl.pallas_call(
        functools.partial(_dil_kernel, r, window // r, slopes),
        grid=(nb, n // DIL_BLOCK, r),
        in_specs=[blockspec(cols[0], False), blockspec(cols[1], False), blockspec(cols[1], True),
                  blockspec(cols[2], False), blockspec(cols[2], True)],
        out_specs=[pl.BlockSpec((pairs, None, DIL_BLOCK * r, LANES), lambda b, i, rho: (0, b, i, 0)),
                   pl.BlockSpec((None, DIL_BLOCK * r, LANES), lambda b, i, rho: (b, i, 0))],
        out_shape=[jax.ShapeDtypeStruct((pairs, nb, seq, LANES), F32),
                   jax.ShapeDtypeStruct((nb, seq, LANES), F32)],
        compiler_params=_cparams("parallel", "parallel", "arbitrary"),
        name=f"dilated_attn_g{g}",
    )(view, view, view, view, view)


def _dil_combine_kernel(l0_ref, l1_ref, l2_ref, o0_ref, o1_ref, o2_ref, out_ref):
    ls = [l0_ref[...], l1_ref[...], l2_ref[...]]
    m = jnp.maximum(jnp.maximum(ls[0], ls[1]), ls[2])
    es = [jnp.exp(l - m) for l in ls]
    den = es[0] + es[1] + es[2]
    e = _head_expander(DIL_HEAD_DIM)
    acc = None
    for ev, o_ref in zip(es, (o0_ref, o1_ref, o2_ref)):
        o = jnp.concatenate([o_ref[p] for p in range(DIL_HEADS // 2)], axis=1)
        term = _expand(ev / den, e) * o
        acc = term if acc is None else acc + term
    out_ref[...] = acc.astype(out_ref.dtype)


def _dilated(pd):
    nb, seq, _ = pd.shape
    res = [_dilated_group(pd, g) for g in range(DIL_GROUPS)]
    t = nb * seq
    tm = 512
    pairs = DIL_HEADS // 2
    lspec = pl.BlockSpec((tm, LANES), lambda i: (i, 0))
    ospec = pl.BlockSpec((pairs, tm, LANES), lambda i: (0, i, 0))
    return pl.pallas_call(
        _dil_combine_kernel,
        grid=(t // tm,),
        in_specs=[lspec] * 3 + [ospec] * 3,
        out_specs=pl.BlockSpec((tm, BRANCH_WIDTH), lambda i: (i, 0)),
        out_shape=jax.ShapeDtypeStruct((t, BRANCH_WIDTH), BF16),
        compiler_params=_cparams("parallel"),
        name="dilated_combine",
    )(*[r[1].reshape(t, LANES) for r in res], *[r[0].reshape(pairs, t, LANES) for r in res])


def _ret_kernel(q_ref, k_ref, v_ref, g_ref, o_ref, st):
    c = CHUNK
    scale = RET_DK ** -0.5

    @pl.when(pl.program_id(1) == 0)
    def _():
        st[...] = jnp.zeros_like(st)

    row = _iota((c, c), 0)
    col = _iota((c, c), 1)
    rel = (row - col).astype(F32)
    pos = row.astype(F32)
    outs = []
    for h in range(RET_HEADS):
        sl = slice(h * RET_DK, (h + 1) * RET_DK)
        lg = math.log(1.0 - 2.0 ** (-5.0 - h))
        decay = jnp.where(row >= col, jnp.exp(lg * jnp.maximum(rel, 0.0)), 0.0) * scale
        qh, kh, vh = q_ref[:, sl], k_ref[:, sl], v_ref[:, sl]
        inner = _mm((_mm_nt(qh, kh) * decay).astype(BF16), vh)
        s_prev = st[h]
        q_dec = (qh.astype(F32) * jnp.exp(lg * (pos + 1.0))).astype(BF16)
        o = inner + _mm(q_dec, s_prev.astype(BF16))
        k_dec = (kh.astype(F32) * (jnp.exp(lg * (c - 1.0 - pos)) * scale)).astype(BF16)
        st[h] = s_prev * math.exp(lg * c) + _mm_tn(k_dec, vh)
        oc = o - jnp.mean(o, axis=-1, keepdims=True)
        outs.append(oc * lax.rsqrt(jnp.mean(oc * oc, axis=-1, keepdims=True) + NORM_EPS))
    o_ref[...] = (jnp.concatenate(outs, axis=1) * _silu(g_ref[...].astype(F32))).astype(o_ref.dtype)


def _retention(pa):
    nb, seq, _ = pa.shape
    spec = lambda col: pl.BlockSpec((None, CHUNK, BRANCH_WIDTH), lambda b, c: (b, c, col))
    return pl.pallas_call(
        _ret_kernel,
        grid=(nb, seq // CHUNK),
        in_specs=[spec(PA_COLS[n]) for n in ("rq", "rk", "rv", "rg")],
        out_specs=spec(0),
        out_shape=jax.ShapeDtypeStruct((nb, seq, BRANCH_WIDTH), BF16),
        scratch_shapes=[pltpu.VMEM((RET_HEADS, RET_DK, RET_DK), F32)],
        compiler_params=_cparams("parallel", "arbitrary"),
        name="retention",
    )(pa, pa, pa, pa)


def _hgrn_kernel(layer, q_ref, f_ref, i_ref, g_ref, lb_ref, nw_ref, o_ref, st):
    c = CHUNK
    width = BRANCH_WIDTH
    neg_inf = -jnp.inf

    @pl.when(pl.program_id(1) == 0)
    def _():
        st[...] = jnp.zeros_like(st)

    rows = [lb_ref[l:l + 1, :] for l in range(DEPTH)]
    mx = functools.reduce(jnp.maximum, rows)
    es = [jnp.exp(r - mx) for r in rows]
    den = functools.reduce(lambda a, b: a + b, es)
    sm = [e / den for e in es]
    lb = functools.reduce(lambda a, b: a + b, sm[:layer + 1]) - sm[0]

    forget = lb + (1.0 - lb) * jax.nn.sigmoid(f_ref[...])
    lf = jnp.log(forget)
    kk = 1.0 - forget
    q = _silu(q_ref[...].astype(F32))
    v16 = i_ref[...]
    vf = v16.astype(F32)
    lam = _cumsum_rows(_tri(c), lf)

    rowi = _iota((c, width), 0)
    sub = jnp.bitwise_and(rowi, HGRN_SUB - 1)
    heads = [slice(h * HGRN_DK, (h + 1) * HGRN_DK) for h in range(HGRN_HEADS)]

    o_acc = [jnp.zeros((c, HGRN_DK), F32) for _ in heads]
    for d in range(HGRN_SUB):
        if d == 0:
            prod = q * kk
            vd = vf
        else:
            ld = pltpu.roll(lam, d, 0)
            prod = q * pltpu.roll(kk, d, 0) * jnp.exp(jnp.where(sub >= d, lam - ld, neg_inf))
            vd = pltpu.roll(vf, d, 0)
        for h, sl in enumerate(heads):
            o_acc[h] = o_acc[h] + jnp.sum(prod[:, sl], axis=-1, keepdims=True) * vd[:, sl]

    ti = _iota((c, c), 0)
    tj = _iota((c, c), 1)
    attn = [jnp.zeros((c, c), F32) for _ in heads]
    size = c
    while size > HGRN_SUB:
        half = size // 2
        ref = jnp.broadcast_to(lam[half - 1:half, :], (c, width))
        for b in range(1, c // size):
            ref = jnp.where(rowi >= b * size, lam[b * size + half - 1:b * size + half, :], ref)
        upper = jnp.bitwise_and(rowi, size - 1) >= half
        qs = (q * jnp.exp(jnp.where(upper, lam - ref, neg_inf))).astype(BF16)
        ks = (kk * jnp.exp(jnp.where(upper, neg_inf, ref - lam))).astype(BF16)
        shift = int(math.log2(size))
        same = jnp.right_shift(ti, shift) == jnp.right_shift(tj, shift)
        for h, sl in enumerate(heads):
            attn[h] = attn[h] + jnp.where(same, _mm_nt(qs[:, sl], ks[:, sl]), 0.0)
        size = half

    lam_last = lam[c - 1:c, :]
    q_in = (q * jnp.exp(lam)).astype(BF16)
    k_out = (kk * jnp.exp(lam_last - lam)).astype(BF16)
    e_last = jnp.exp(lam_last)
    outs = []
    for h, sl in enumerate(heads):
        s_prev = st[h]
        o = o_acc[h] + _mm(attn[h].astype(BF16), v16[:, sl]) + _mm_nt(q_in[:, sl], s_prev.astype(BF16))
        st[h] = s_prev * e_last[:, sl] + _mm_tn(v16[:, sl], k_out[:, sl])
        outs.append(o * lax.rsqrt(jnp.mean(o * o, axis=-1, keepdims=True) + NORM_EPS) * nw_ref[...])
    o_ref[...] = (jnp.concatenate(outs, axis=1) * _silu(g_ref[...].astype(F32))).astype(o_ref.dtype)


def _hgrn(pa, pb, layer, hgrn_lb, norm_w):
    nb, seq, _ = pa.shape
    spec = lambda col: pl.BlockSpec((None, CHUNK, BRANCH_WIDTH), lambda b, c: (b, c, col))
    lb = hgrn_lb.astype(F32)
    nw = norm_w.reshape(1, HGRN_DK).astype(F32)
    return pl.pallas_call(
        functools.partial(_hgrn_kernel, layer),
        grid=(nb, seq // CHUNK),
        in_specs=[spec(PA_COLS["hq"]), spec(PB_HF_COL), spec(PA_COLS["hi"]), spec(PA_COLS["hg"]),
                  pl.BlockSpec(lb.shape, lambda b, c: (0, 0)), pl.BlockSpec(nw.shape, lambda b, c: (0, 0))],
        out_specs=spec(0),
        out_shape=jax.ShapeDtypeStruct((nb, seq, BRANCH_WIDTH), BF16),
        scratch_shapes=[pltpu.VMEM((HGRN_HEADS, HGRN_DK, HGRN_DK), F32)],
        compiler_params=_cparams("parallel", "arbitrary"),
        name="hgrn2",
    )(pa, pb, pa, pa, lb, nw)


def _merge_kernel(x_ref, mod_ref, o0_ref, o1_ref, o2_ref, o3_ref, wm_ref, wb_ref, wo_ref, g_ref, b_ref, out_ref):
    m = mod_ref[...]
    x = x_ref[...]
    h = (x * (1.0 + m[1:2]) + m[0:1]).astype(BF16)
    y = None
    for i, o_ref in enumerate((o0_ref, o1_ref, o2_ref, o3_ref)):
        gate = jax.nn.sigmoid(_mm(h, wm_ref[:, i * D_MODEL:(i + 1) * D_MODEL]))
        term = gate * _mm(o_ref[...], wb_ref[i])
        y = term if y is None else y + term
    mix = _mm(y.astype(BF16), wo_ref[...])
    out_ref[...] = _layer_norm_rows(DEEPNORM_ALPHA * x + m[2:3] * mix, g_ref[...], b_ref[...])


def _merge(x2, mod, branches, w_merge, w_branch, w_o, ln_g, ln_b, seq):
    t = x2.shape[0]
    tm = 256
    per_b = seq // tm
    rows = pl.BlockSpec((tm, D_MODEL), lambda i: (i, 0))
    brow = pl.BlockSpec((tm, BRANCH_WIDTH), lambda i: (i, 0))
    return pl.pallas_call(
        _merge_kernel,
        grid=(t // tm,),
        in_specs=[rows, pl.BlockSpec((None, 6, D_MODEL), lambda i: (i // per_b, 0, 0))] + [brow] * N_BRANCH
        + [_const_spec(w_merge.shape), _const_spec(w_branch.shape), _const_spec(w_o.shape),
           _const_spec((1, D_MODEL)), _const_spec((1, D_MODEL))],
        out_specs=rows,
        out_shape=jax.ShapeDtypeStruct((t, D_MODEL), F32),
        compiler_params=_cparams("parallel"),
        name="merge_out_ln",
    )(x2, mod, *branches, w_merge, w_branch, w_o, ln_g.reshape(1, -1), ln_b.reshape(1, -1))


def _ffn_kernel(x_ref, mod_ref, wu_ref, cw_ref, cb_ref, wd_ref, g_ref, b_ref, out_ref, ubuf0, ubuf1, tail, act):
    tm = x_ref.shape[0]
    halo = 8
    assert N_FF_CHUNKS % 2 == 1

    @pl.when(pl.program_id(1) == 0)
    def _():
        tail[...] = jnp.zeros_like(tail)

    m = mod_ref[...]
    x = x_ref[...]
    h = (x * (1.0 + m[4:5]) + m[3:4]).astype(BF16)

    def up(j, ubuf):
        ubuf[0:halo, :] = tail[j]
        ubuf[halo:halo + tm, :] = _mm(h, wu_ref[j])
        tail[j] = ubuf[tm:tm + halo, :]

    def gate(j, ubuf):
        cw = cw_ref[j]
        u = jnp.broadcast_to(cb_ref[j], (tm, 2 * FF_CHUNK))
        for k in range(FFN_CONV):
            off = halo - (FFN_CONV - 1) + k
            u = u + cw[k:k + 1, :] * ubuf[off:off + tm, :]
        act[j] = (_silu(u[:, :FF_CHUNK]) * u[:, FF_CHUNK:]).astype(BF16)

    bufs = (ubuf0, ubuf1)
    up(0, ubuf0)
    for j in range(N_FF_CHUNKS):
        if j + 1 < N_FF_CHUNKS:
            up(j + 1, bufs[(j + 1) % 2])
        gate(j, bufs[j % 2])
    ffn = _mm(act[0], wd_ref[0])
    for j in range(1, N_FF_CHUNKS):
        ffn = ffn + _mm(act[j], wd_ref[j])
    out_ref[...] = _layer_norm_rows(DEEPNORM_ALPHA * x + m[5:6] * ffn, g_ref[...], b_ref[...])


def _ffn(x2, mod, w_up, conv_w, conv_b, w_down, ln_g, ln_b, nb, seq):
    tm = 256
    per_b = seq // tm
    rows = pl.BlockSpec((tm, D_MODEL), lambda b, i: (b * per_b + i, 0))
    return pl.pallas_call(
        _ffn_kernel,
        grid=(nb, per_b),
        in_specs=[rows, pl.BlockSpec((None, 6, D_MODEL), lambda b, i: (b, 0, 0)),
                  _const_spec(w_up.shape), _const_spec(conv_w.shape), _const_spec(conv_b.shape),
                  _const_spec(w_down.shape), _const_spec((1, D_MODEL)), _const_spec((1, D_MODEL))],
        out_specs=rows,
        out_shape=jax.ShapeDtypeStruct((nb * seq, D_MODEL), F32),
        scratch_shapes=[pltpu.VMEM((tm + 8, 2 * FF_CHUNK), F32),
                        pltpu.VMEM((tm + 8, 2 * FF_CHUNK), F32),
                        pltpu.VMEM((N_FF_CHUNKS, 8, 2 * FF_CHUNK), F32),
                        pltpu.VMEM((N_FF_CHUNKS, tm, FF_CHUNK), BF16)],
        compiler_params=_cparams("parallel", "arbitrary"),
        name="conv_ffn_ln",
    )(x2, mod, w_up, conv_w, conv_b, w_down, ln_g.reshape(1, -1), ln_b.reshape(1, -1))


def _pack_in_proj(w_in):
    offs = np.concatenate([[0], np.cumsum(SECTION_SIZES)])
    sec = {n: w_in[:, offs[i]:offs[i + 1]] for i, n in enumerate(SECTION_NAMES)}
    w = BRANCH_WIDTH
    sec.update({f"q{g}": sec["dq"][:, g * w:(g + 1) * w] for g in range(DIL_GROUPS)}, k=sec["dk"], v=sec["dv"])
    wa = jnp.concatenate([sec[n] for n in sorted(PA_COLS, key=PA_COLS.get)], axis=1)
    wd = jnp.concatenate([sec[n] for n in sorted(PD_COLS, key=PD_COLS.get)], axis=1)
    dt_pad = jnp.zeros((D_MODEL, 2 * LANES - SSD_HEADS), w_in.dtype)
    wb = jnp.concatenate([sec["hf"], sec["dt"], dt_pad, sec["xbc"]], axis=1)
    return wa.astype(BF16), wd.astype(BF16), wb.astype(BF16)


def _chunk_gate_value(a):
    lead = a.shape[:-1]
    a = a.reshape(lead + (2, N_FF_CHUNKS, FF_CHUNK))
    a = jnp.moveaxis(a, -2, 0)
    return a.reshape((N_FF_CHUNKS,) + lead + (2 * FF_CHUNK,))


def kernel(x, c, w_ada, b_ada, w_in, w_merge, ssd_conv_w, ssd_conv_b, ssd_dt_bias, ssd_a_log, ssd_d, ssd_norm_w, hgrn_lb, hgrn_norm_w, w_branch_out, w_o, ln1_g, ln1_b, w_up, ffn_conv_w, ffn_conv_b, w_down, ln2_g, ln2_b):
    nb, seq, _ = x.shape
    assert seq % (DIL_PATTERNS[-1][1] * DIL_BLOCK) == 0 and x.shape[-1] == D_MODEL
    mods = _modulation(c, w_ada, b_ada).reshape(DEPTH, nb, 6, D_MODEL)
    x2 = x.reshape(nb * seq, D_MODEL)
    for layer in range(DEPTH):
        mod = mods[layer]
        pa, pd, pb = (a.reshape(nb, seq, -1) for a in _project(x2, mod, *_pack_in_proj(w_in[layer]), seq))
        o_ssd = _ssd(pa, pb, ssd_conv_w[layer], ssd_conv_b[layer], ssd_dt_bias[layer], ssd_a_log[layer],
                     ssd_d[layer], ssd_norm_w[layer])
        o_dil = _dilated(pd)
        o_ret = _retention(pa)
        o_hgrn = _hgrn(pa, pb, layer, hgrn_lb, hgrn_norm_w[layer])
        flat = lambda a: a.reshape(nb * seq, BRANCH_WIDTH)
        x2 = _merge(x2, mod, (flat(o_ssd), o_dil, flat(o_ret), flat(o_hgrn)), w_merge[layer].astype(BF16),
                    w_branch_out[layer].astype(BF16), w_o[layer].astype(BF16), ln1_g[layer], ln1_b[layer], seq)
        x2 = _ffn(x2, mod, _chunk_gate_value(w_up[layer]).astype(BF16), _chunk_gate_value(ffn_conv_w[layer]),
                  _chunk_gate_value(ffn_conv_b[layer].reshape(1, -1)), w_down[layer].reshape(N_FF_CHUNKS, FF_CHUNK, D_MODEL).astype(BF16),
                  ln2_g[layer], ln2_b[layer], nb, seq)
    return x2.reshape(nb, seq, D_MODEL)
```

```python
import functools
import math

import numpy as np
import jax
import jax.numpy as jnp
from jax import lax
from jax.experimental import pallas as pl
from jax.experimental.pallas import tpu as pltpu

F32 = jnp.float32
BF16 = jnp.bfloat16

D_MODEL = 1024
DEPTH = 2
DEEPNORM_ALPHA = (2 * DEPTH) ** 0.25
NORM_EPS = 1e-5

N_BRANCH = 4
BRANCH_WIDTH = D_MODEL // 2

SSD_HEAD_DIM = 64
SSD_HEADS = BRANCH_WIDTH // SSD_HEAD_DIM
SSD_GROUPS = 2
SSD_STATE = 64
SSD_CONV = 4
SSD_CONV_DIM = BRANCH_WIDTH + 2 * SSD_GROUPS * SSD_STATE

DIL_HEAD_DIM = 64
DIL_HEADS = BRANCH_WIDTH // DIL_HEAD_DIM
DIL_PATTERNS = ((128, 1), (512, 4), (2048, 16))
DIL_GROUPS = len(DIL_PATTERNS)
DIL_BLOCK = 128

RET_HEADS = 4
RET_DK = BRANCH_WIDTH // RET_HEADS

HGRN_HEADS = 4
HGRN_DK = BRANCH_WIDTH // HGRN_HEADS
HGRN_SUB = 4

D_FF = 256 * ((8 * D_MODEL // 3 + 255) // 256)
FFN_CONV = 3
FF_CHUNK = 256
N_FF_CHUNKS = D_FF // FF_CHUNK

SECTION_SIZES = (
    BRANCH_WIDTH, SSD_CONV_DIM, SSD_HEADS,
    DIL_GROUPS * BRANCH_WIDTH, BRANCH_WIDTH, BRANCH_WIDTH,
    BRANCH_WIDTH, BRANCH_WIDTH, BRANCH_WIDTH, BRANCH_WIDTH,
    BRANCH_WIDTH, BRANCH_WIDTH, BRANCH_WIDTH, BRANCH_WIDTH,
)
SECTION_NAMES = ("z", "xbc", "dt", "dq", "dk", "dv", "rq", "rk", "rv", "rg", "hq", "hf", "hi", "hg")

LANES = 128
CHUNK = 128
PA_COLS = {n: i for i, n in enumerate(("rq", "rk", "rv", "rg", "hq", "hi", "hg", "z"))}
PD_COLS = {n: i for i, n in enumerate(("q0", "q1", "k", "v", "q2"))}
PB_HF_COL, PB_DT_COL, PB_XBC_COL = 0, 4, 1
VMEM_LIMIT = 56 * 1024 * 1024


def _cparams(*sem):
    return pltpu.CompilerParams(dimension_semantics=sem, vmem_limit_bytes=VMEM_LIMIT)


def _silu(v):
    return v * jax.nn.sigmoid(v)


def _mm(a, b):
    return jnp.dot(a, b, preferred_element_type=F32)


def _mm_nt(a, b):
    return lax.dot_general(a, b, (((1,), (1,)), ((), ())), preferred_element_type=F32)


def _mm_tn(a, b):
    return lax.dot_general(a, b, (((0,), (0,)), ((), ())), preferred_element_type=F32)


def _split2(a):
    hi = a.astype(BF16)
    lo = (a - hi.astype(F32)).astype(BF16)
    return hi, lo


def _split3(a):
    hi = a.astype(BF16)
    r = a - hi.astype(F32)
    mid = r.astype(BF16)
    lo = (r - mid.astype(F32)).astype(BF16)
    return hi, mid, lo


def _mm_f32(a, b):
    ah, al = _split2(a)
    bh, bl = _split2(b)
    return _mm(ah, bh) + (_mm(ah, bl) + _mm(al, bh))


def _cumsum_rows(tri, a):
    hi, mid, lo = _split3(a)
    return _mm(tri, hi) + (_mm(tri, mid) + _mm(tri, lo))


def _expand(a, e):
    hi, lo = _split2(a)
    return _mm(hi, e) + _mm(lo, e)


def _iota(shape, dim):
    return lax.broadcasted_iota(jnp.int32, shape, dim)


def _tri(n):
    return jnp.where(_iota((n, n), 1) <= _iota((n, n), 0), 1.0, 0.0).astype(BF16)


def _head_expander(width):
    shape = (LANES, BRANCH_WIDTH)
    shift = int(math.log2(width))
    return jnp.where(jnp.right_shift(_iota(shape, 1), shift) == _iota(shape, 0), 1.0, 0.0).astype(BF16)


def _layer_norm_rows(v, g, b):
    vc = v - jnp.mean(v, axis=-1, keepdims=True)
    return vc * lax.rsqrt(jnp.mean(vc * vc, axis=-1, keepdims=True) + NORM_EPS) * g + b


def _mod_kernel(c_ref, w_ref, b_ref, o_ref):
    o_ref[0] = _mm_f32(_silu(c_ref[...]), w_ref[0]) + b_ref[0]


def _modulation(c, w_ada, b_ada):
    nb = c.shape[0]
    tn = 1536
    return pl.pallas_call(
        _mod_kernel,
        grid=(DEPTH, 6 * D_MODEL // tn),
        in_specs=[pl.BlockSpec((nb, D_MODEL), lambda l, j: (0, 0)),
                  pl.BlockSpec((1, D_MODEL, tn), lambda l, j: (l, 0, j)),
                  pl.BlockSpec((1, 1, tn), lambda l, j: (l, 0, j))],
        out_specs=pl.BlockSpec((1, nb, tn), lambda l, j: (l, 0, j)),
        out_shape=jax.ShapeDtypeStruct((DEPTH, nb, 6 * D_MODEL), F32),
        compiler_params=_cparams("parallel", "parallel"),
        name="adaln_mod",
    )(c, w_ada, b_ada.reshape(DEPTH, 1, 6 * D_MODEL))


def _const_spec(shape):
    zeros = (0,) * len(shape)
    return pl.BlockSpec(shape, lambda *_: zeros, pipeline_mode=pl.Buffered(1))


def _destride_matrix(n, r):
    per = n // r
    i = _iota((n, n), 0)
    src = jnp.left_shift(jnp.bitwise_and(i, per - 1), int(math.log2(r))) + jnp.right_shift(i, int(math.log2(per)))
    return jnp.where(_iota((n, n), 1) == src, 1.0, 0.0).astype(BF16)


def _proj_kernel(x_ref, mod_ref, wa_ref, wd_ref, wb_ref, oa_ref, od0_ref, od1_ref, od2_ref, ob_ref):
    m = mod_ref[...]
    h = (x_ref[...] * (1.0 + m[1:2]) + m[0:1]).astype(BF16)
    oa_ref[...] = _mm(h, wa_ref[...]).astype(oa_ref.dtype)
    ob_ref[...] = _mm(h, wb_ref[...]).astype(ob_ref.dtype)
    d = _mm(h, wd_ref[...]).astype(BF16)
    w = BRANCH_WIDTH
    tm = d.shape[0]
    od0_ref[...] = jnp.concatenate([d[:, 0:w], d[:, 2 * w:4 * w]], axis=1)
    od1_ref[...] = _mm(_destride_matrix(tm, DIL_PATTERNS[1][1]), d[:, w:4 * w]).astype(BF16)
    od2_ref[...] = _mm(_destride_matrix(tm, DIL_PATTERNS[2][1]), d[:, 2 * w:5 * w]).astype(BF16)


PROJ_ROWS = 256
DIL_OPERAND_COLS = ((0, 1, 2), (0, 1, 2), (2, 0, 1))


def _project(x2, mod, wa, wd, wb, seq):
    t = x2.shape[0]
    tm = PROJ_ROWS
    per_b = seq // tm
    qkv = 3 * BRANCH_WIDTH
    outs = [(wa.shape[1], BF16), (qkv, BF16), (qkv, BF16), (qkv, BF16), (wb.shape[1], F32)]
    return pl.pallas_call(
        _proj_kernel,
        grid=(t // tm,),
        in_specs=[pl.BlockSpec((tm, D_MODEL), lambda i: (i, 0)),
                  pl.BlockSpec((None, 6, D_MODEL), lambda i: (i // per_b, 0, 0)),
                  _const_spec(wa.shape), _const_spec(wd.shape), _const_spec(wb.shape)],
        out_specs=[pl.BlockSpec((tm, n), lambda i: (i, 0)) for n, _ in outs],
        out_shape=[jax.ShapeDtypeStruct((t, n), dt) for n, dt in outs],
        compiler_params=_cparams("parallel"),
        name="in_proj",
    )(x2, mod, wa, wd, wb)


def _ssd_kernel(z_ref, xbc_ref, dt_ref, cw_ref, cb_ref, dtb_ref, alog_ref, dsk_ref, nw_ref, o_ref, xbuf, st):
    c = CHUNK
    halo = 8

    @pl.when(pl.program_id(1) == 0)
    def _():
        xbuf[0:halo, :] = jnp.zeros((halo, SSD_CONV_DIM), F32)
        st[...] = jnp.zeros_like(st)

    xbuf[halo:halo + c, :] = xbc_ref[...]
    acc = jnp.broadcast_to(cb_ref[...], (c, SSD_CONV_DIM))
    for k in range(SSD_CONV):
        off = halo - (SSD_CONV - 1) + k
        acc = acc + cw_ref[k:k + 1, :] * xbuf[off:off + c, :]
    xbuf[0:halo, :] = xbuf[c:c + halo, :]
    y = _silu(acc)
    xs = y[:, :BRANCH_WIDTH]
    bm = y[:, BRANCH_WIDTH:BRANCH_WIDTH + LANES]
    cm = y[:, BRANCH_WIDTH + LANES:]

    dtr = dt_ref[...] + dtb_ref[...]
    dt = jnp.maximum(dtr, 0.0) + jnp.log1p(jnp.exp(-jnp.abs(dtr)))
    da = dt * (-jnp.exp(alog_ref[...]))
    cs = _cumsum_rows(_tri(c), da)
    cs_t = cs.T
    tot = cs[c - 1:c, :]

    stack = jnp.concatenate([dt, jnp.exp(tot - cs), jnp.exp(cs), jnp.broadcast_to(jnp.exp(tot), (8, LANES))], axis=0)
    ex = _expand(stack, _head_expander(SSD_HEAD_DIM))
    dt_e, ds_e, ecs_e, dec_e = ex[0:c], ex[c:2 * c], ex[2 * c:3 * c], ex[3 * c:3 * c + 1]

    xdt = xs * dt_e
    xds = xdt * ds_e
    causal = _iota((c, c), 1) <= _iota((c, c), 0)
    lane = _iota((c, LANES), 1)
    bm16 = bm.astype(BF16)
    cbs = []
    for g in range(SSD_GROUPS):
        cm_g = jnp.where(jnp.right_shift(lane, 6) == g, cm, 0.0).astype(BF16)
        cbs.append(_mm_nt(cm_g, bm16))
    parts = []
    for p in range(SSD_HEADS // 2):
        g = (2 * p) // (SSD_HEADS // SSD_GROUPS)
        ms = []
        for e in range(2):
            h = 2 * p + e
            diff = cs[:, h:h + 1] - cs_t[h:h + 1, :]
            ms.append((cbs[g] * jnp.exp(jnp.where(causal, diff, -jnp.inf))).astype(BF16))
        xp = xdt[:, p * LANES:(p + 1) * LANES]
        xbd = jnp.concatenate([jnp.where(lane < SSD_HEAD_DIM, xp, 0.0), jnp.where(lane >= SSD_HEAD_DIM, xp, 0.0)],
                              axis=0).astype(BF16)
        parts.append(_mm(jnp.concatenate(ms, axis=1), xbd))
    y_diag = jnp.concatenate(parts, axis=1)

    s_prev = st[...]
    y_off = _mm(cm.astype(BF16), s_prev.astype(BF16)) * ecs_e
    upd = _mm(bm.T.astype(BF16), xds.astype(BF16))
    shape = (LANES, BRANCH_WIDTH)
    same_group = jnp.right_shift(_iota(shape, 0), 6) == jnp.right_shift(_iota(shape, 1), 8)
    st[...] = s_prev * dec_e + jnp.where(same_group, upd, 0.0)

    yv = (y_diag + y_off + xs * dsk_ref[...]) * _silu(z_ref[...].astype(F32))
    gw = BRANCH_WIDTH // SSD_GROUPS
    outs = []
    for g in range(SSD_GROUPS):
        yg = yv[:, g * gw:(g + 1) * gw]
        outs.append(yg * lax.rsqrt(jnp.mean(yg * yg, axis=-1, keepdims=True) + NORM_EPS))
    o_ref[...] = (jnp.concatenate(outs, axis=1) * nw_ref[...]).astype(o_ref.dtype)


def _ssd(pa, pb, conv_w, conv_b, dt_bias, a_log, d_skip, norm_w):
    nb, seq, _ = pa.shape
    pad = LANES - SSD_HEADS
    row = lambda v: v.reshape(1, -1).astype(F32)
    args = (conv_w.astype(F32), row(conv_b), row(jnp.pad(dt_bias, (0, pad))), row(jnp.pad(a_log, (0, pad))),
            row(jnp.repeat(d_skip, SSD_HEAD_DIM)), row(norm_w))
    full = lambda a: pl.BlockSpec(a.shape, lambda b, c: (0, 0))
    return pl.pallas_call(
        _ssd_kernel,
        grid=(nb, seq // CHUNK),
        in_specs=[pl.BlockSpec((None, CHUNK, BRANCH_WIDTH), lambda b, c: (b, c, PA_COLS["z"])),
                  pl.BlockSpec((None, CHUNK, SSD_CONV_DIM), lambda b, c: (b, c, PB_XBC_COL)),
                  pl.BlockSpec((None, CHUNK, LANES), lambda b, c: (b, c, PB_DT_COL))] + [full(a) for a in args],
        out_specs=pl.BlockSpec((None, CHUNK, BRANCH_WIDTH), lambda b, c: (b, c, 0)),
        out_shape=jax.ShapeDtypeStruct((nb, seq, BRANCH_WIDTH), BF16),
        scratch_shapes=[pltpu.VMEM((CHUNK + 8, SSD_CONV_DIM), F32), pltpu.VMEM((LANES, BRANCH_WIDTH), F32)],
        compiler_params=_cparams("parallel", "arbitrary"),
        name="ssd_mixer",
    )(pa, pb, pb, *args)


def _alibi_slopes(n):
    def pow2(k):
        start = 2.0 ** (-8.0 / k)
        return [start ** (i + 1) for i in range(k)]
    if math.log2(n).is_integer():
        s = pow2(n)
    else:
        c = 2 ** math.floor(math.log2(n))
        s = pow2(c) + pow2(2 * c)[0::2][: n - c]
    return [float(np.float32(v)) for v in s]


def _dil_kernel(dilation, n_back, slopes, q_ref, kc_ref, kp_ref, vc_ref, vp_ref, o_ref, lse_ref):
    blk = DIL_BLOCK
    rho = pl.program_id(2)
    log2e = 1.0 / math.log(2.0)

    def rows_of(ref, start, sl):
        if len(ref.shape) == 2:
            return ref[start:start + blk, sl]
        per = ref.shape[1]
        return jnp.concatenate([ref[t, :, sl] for t in range(start // per, (start + blk) // per)], axis=0)

    qi = _iota((blk, 2 * blk), 0)
    kj = _iota((blk, 2 * blk), 1)
    dist = qi - kj + blk
    in_window = (dist >= 0) & (dist <= n_back)
    neg_dist2 = (dist * dilation).astype(F32) * (-log2e)
    lane = _iota((blk, LANES), 1)
    for sb in range(DIL_QBLOCKS):
        if sb == 0:
            valid = in_window & (kj >= jnp.where(pl.program_id(1) > 0, 0, blk))
        else:
            valid = in_window
        bias = jnp.where(valid, neg_dist2, -jnp.inf)
        rows = pl.ds(rho + sb * blk * dilation, blk, stride=dilation) if dilation > 1 else pl.ds(sb * blk, blk)
        lse_tile = jnp.zeros((blk, LANES), F32)
        for p in range(DIL_HEADS // 2):
            sl = slice(p * LANES, (p + 1) * LANES)
            qp = rows_of(q_ref, sb * blk, sl).astype(F32) * (DIL_HEAD_DIM ** -0.5 * log2e)
            if sb == 0:
                kk = jnp.concatenate([rows_of(kp_ref, 0, sl), rows_of(kc_ref, 0, sl)], axis=0)
                vv = jnp.concatenate([rows_of(vp_ref, 0, sl), rows_of(vc_ref, 0, sl)], axis=0)
            else:
                kk = jnp.concatenate([rows_of(kc_ref, (sb - 1) * blk, sl), rows_of(kc_ref, sb * blk, sl)], axis=0)
                vv = jnp.concatenate([rows_of(vc_ref, (sb - 1) * blk, sl), rows_of(vc_ref, sb * blk, sl)], axis=0)
            pair = []
            for e in range(2):
                h = 2 * p + e
                own = (lane < DIL_HEAD_DIM) if e == 0 else (lane >= DIL_HEAD_DIM)
                qm = jnp.where(own, qp, 0.0).astype(BF16)
                s = _mm_nt(qm, kk) + slopes[h] * bias
                m = jnp.max(s, axis=-1, keepdims=True)
                pexp = jnp.exp2(s - m)
                l = jnp.sum(pexp, axis=-1, keepdims=True)
                pair.append(_mm(pexp.astype(BF16), vv) / l)
                lse_tile = jnp.where(lane == h, (m + jnp.log2(l)) * math.log(2.0), lse_tile)
            o_ref[p, rows, :] = jnp.where(lane < DIL_HEAD_DIM, pair[0], pair[1])
        lse_ref[rows, :] = lse_tile


DIL_QBLOCKS = 2


def _dilated_group(pd, g):
    nb, seq, _ = pd.shape
    window, r = DIL_PATTERNS[g]
    n = seq // r
    qrows = DIL_QBLOCKS * DIL_BLOCK
    w = BRANCH_WIDTH
    qc, kc, vc = DIL_OPERAND_COLS[g]
    if r == 1:
        view = pd
        cur = lambda col: pl.BlockSpec((None, qrows, w), lambda b, i, rho: (b, i, col))
        prev = lambda col: pl.BlockSpec((None, DIL_BLOCK, w),
                                        lambda b, i, rho: (b, jnp.maximum(DIL_QBLOCKS * i - 1, 0), col))
    else:
        per = PROJ_ROWS // r
        view = pd.reshape(nb, seq // PROJ_ROWS, r, per, 3 * w)
        cur = lambda col: pl.BlockSpec((None, qrows // per, None, per, w), lambda b, i, rho: (b, i, rho, 0, col))
        prev = lambda col: pl.BlockSpec((None, DIL_BLOCK // per, None, per, w),
                                        lambda b, i, rho: (b, jnp.maximum(DIL_QBLOCKS * i - 1, 0), rho, 0, col))
    slopes = _alibi_slopes(DIL_GROUPS * DIL_HEADS)[g * DIL_HEADS:(g + 1) * DIL_HEADS]
    pairs = DIL_HEADS // 2
    return pl.pallas_call(
        functools.partial(_dil_kernel, r, window // r, slopes),
        grid=(nb, n // qrows, r),
        in_specs=[cur(qc), cur(kc), prev(kc), cur(vc), prev(vc)],
        out_specs=[pl.BlockSpec((pairs, None, qrows * r, LANES), lambda b, i, rho: (0, b, i, 0)),
                   pl.BlockSpec((None, qrows * r, LANES), lambda b, i, rho: (b, i, 0))],
        out_shape=[jax.ShapeDtypeStruct((pairs, nb, seq, LANES), F32),
                   jax.ShapeDtypeStruct((nb, seq, LANES), F32)],
        compiler_params=_cparams("parallel", "parallel", "arbitrary"),
        name=f"dilated_attn_g{g}",
    )(view, view, view, view, view)


def _dil_combine_kernel(l0_ref, l1_ref, l2_ref, o0_ref, o1_ref, o2_ref, out_ref):
    ls = [l0_ref[...], l1_ref[...], l2_ref[...]]
    m = jnp.maximum(jnp.maximum(ls[0], ls[1]), ls[2])
    es = [jnp.exp(l - m) for l in ls]
    den = es[0] + es[1] + es[2]
    e = _head_expander(DIL_HEAD_DIM)
    acc = None
    for ev, o_ref in zip(es, (o0_ref, o1_ref, o2_ref)):
        o = jnp.concatenate([o_ref[p] for p in range(DIL_HEADS // 2)], axis=1)
        term = _expand(ev / den, e) * o
        acc = term if acc is None else acc + term
    out_ref[...] = acc.astype(out_ref.dtype)


def _dilated(pds):
    nb, seq, _ = pds[0].shape
    res = [_dilated_group(pd, g) for g, pd in enumerate(pds)]
    t = nb * seq
    tm = 512
    pairs = DIL_HEADS // 2
    lspec = pl.BlockSpec((tm, LANES), lambda i: (i, 0))
    ospec = pl.BlockSpec((pairs, tm, LANES), lambda i: (0, i, 0))
    return pl.pallas_call(
        _dil_combine_kernel,
        grid=(t // tm,),
        in_specs=[lspec] * 3 + [ospec] * 3,
        out_specs=pl.BlockSpec((tm, BRANCH_WIDTH), lambda i: (i, 0)),
        out_shape=jax.ShapeDtypeStruct((t, BRANCH_WIDTH), BF16),
        compiler_params=_cparams("parallel"),
        name="dilated_combine",
    )(*[r[1].reshape(t, LANES) for r in res], *[r[0].reshape(pairs, t, LANES) for r in res])


def _ret_kernel(q_ref, k_ref, v_ref, g_ref, o_ref, st):
    c = CHUNK
    scale = RET_DK ** -0.5

    @pl.when(pl.program_id(1) == 0)
    def _():
        st[...] = jnp.zeros_like(st)

    row = _iota((c, c), 0)
    col = _iota((c, c), 1)
    rel = (row - col).astype(F32)
    pos = row.astype(F32)
    for b in range(q_ref.shape[0]):
        outs = []
        for h in range(RET_HEADS):
            sl = slice(h * RET_DK, (h + 1) * RET_DK)
            lg = math.log(1.0 - 2.0 ** (-5.0 - h))
            decay = jnp.where(row >= col, jnp.exp(lg * jnp.maximum(rel, 0.0)), 0.0) * scale
            qh, kh, vh = q_ref[b, :, sl], k_ref[b, :, sl], v_ref[b, :, sl]
            inner = _mm((_mm_nt(qh, kh) * decay).astype(BF16), vh)
            s_prev = st[b, h]
            q_dec = (qh.astype(F32) * jnp.exp(lg * (pos + 1.0))).astype(BF16)
            o = inner + _mm(q_dec, s_prev.astype(BF16))
            k_dec = (kh.astype(F32) * (jnp.exp(lg * (c - 1.0 - pos)) * scale)).astype(BF16)
            st[b, h] = s_prev * math.exp(lg * c) + _mm_tn(k_dec, vh)
            oc = o - jnp.mean(o, axis=-1, keepdims=True)
            outs.append(oc * lax.rsqrt(jnp.mean(oc * oc, axis=-1, keepdims=True) + NORM_EPS))
        o_ref[b] = (jnp.concatenate(outs, axis=1) * _silu(g_ref[b].astype(F32))).astype(o_ref.dtype)


RET_ROWS = 2


def _retention(pa):
    nb, seq, _ = pa.shape
    rows = RET_ROWS if nb % RET_ROWS == 0 else 1
    spec = lambda col: pl.BlockSpec((rows, CHUNK, BRANCH_WIDTH), lambda b, c: (b, c, col))
    return pl.pallas_call(
        _ret_kernel,
        grid=(nb // rows, seq // CHUNK),
        in_specs=[spec(PA_COLS[n]) for n in ("rq", "rk", "rv", "rg")],
        out_specs=spec(0),
        out_shape=jax.ShapeDtypeStruct((nb, seq, BRANCH_WIDTH), BF16),
        scratch_shapes=[pltpu.VMEM((rows, RET_HEADS, RET_DK, RET_DK), F32)],
        compiler_params=_cparams("parallel", "arbitrary"),
        name="retention",
    )(pa, pa, pa, pa)


def _hgrn_kernel(layer, q_ref, f_ref, i_ref, g_ref, lb_ref, nw_ref, o_ref, st):
    c = CHUNK
    width = BRANCH_WIDTH
    neg_inf = -jnp.inf

    @pl.when(pl.program_id(1) == 0)
    def _():
        st[...] = jnp.zeros_like(st)

    rows = [lb_ref[l:l + 1, :] for l in range(DEPTH)]
    mx = functools.reduce(jnp.maximum, rows)
    es = [jnp.exp(r - mx) for r in rows]
    den = functools.reduce(lambda a, b: a + b, es)
    sm = [e / den for e in es]
    lb = functools.reduce(lambda a, b: a + b, sm[:layer + 1]) - sm[0]

    forget = lb + (1.0 - lb) * jax.nn.sigmoid(f_ref[...])
    lf = jnp.log(forget)
    kk = 1.0 - forget
    q = _silu(q_ref[...].astype(F32))
    v16 = i_ref[...]
    vf = v16.astype(F32)
    ti = _iota((c, c), 0)
    tj = _iota((c, c), 1)
    le_t = jnp.where(tj <= ti, 1.0, 0.0)
    sizes = [c >> k for k in range(int(math.log2(c // HGRN_SUB)))]
    mats = [le_t]
    for size in sizes:
        sh = int(math.log2(size))
        last_lower = jnp.left_shift(jnp.right_shift(ti, sh), sh) + (size // 2 - 1)
        mats.append(le_t - jnp.where(tj <= last_lower, 1.0, 0.0))
    sums = _cumsum_rows(jnp.concatenate(mats, axis=0).astype(BF16), lf)
    lam = sums[0:c]

    rowi = _iota((c, width), 0)
    sub = jnp.bitwise_and(rowi, HGRN_SUB - 1)
    heads = [slice(h * HGRN_DK, (h + 1) * HGRN_DK) for h in range(HGRN_HEADS)]

    o_acc = [jnp.zeros((c, HGRN_DK), F32) for _ in heads]
    for d in range(HGRN_SUB):
        if d == 0:
            prod = q * kk
            vd = vf
        else:
            ld = pltpu.roll(lam, d, 0)
            prod = q * pltpu.roll(kk, d, 0) * jnp.exp(jnp.where(sub >= d, lam - ld, neg_inf))
            vd = pltpu.roll(vf, d, 0)
        for h, sl in enumerate(heads):
            o_acc[h] = o_acc[h] + jnp.sum(prod[:, sl], axis=-1, keepdims=True) * vd[:, sl]

    attn = [jnp.zeros((c, c), F32) for _ in heads]
    for k, size in enumerate(sizes):
        half = size // 2
        rel = sums[(k + 1) * c:(k + 2) * c]
        upper = jnp.bitwise_and(rowi, size - 1) >= half
        z = (jnp.where(upper, q, kk) * jnp.exp(-jnp.abs(rel))).astype(BF16)
        sh = int(math.log2(size))
        pairs = ((jnp.right_shift(ti, sh) == jnp.right_shift(tj, sh))
                 & (jnp.bitwise_and(ti, size - 1) >= half) & (jnp.bitwise_and(tj, size - 1) < half))
        for h, sl in enumerate(heads):
            attn[h] = attn[h] + jnp.where(pairs, _mm_nt(z[:, sl], z[:, sl]), 0.0)

    lam_last = lam[c - 1:c, :]
    q_in = (q * jnp.exp(lam)).astype(BF16)
    k_out = (kk * jnp.exp(lam_last - lam)).astype(BF16)
    e_last = jnp.exp(lam_last)
    outs = []
    for h, sl in enumerate(heads):
        s_prev = st[h]
        o = o_acc[h] + _mm(attn[h].astype(BF16), v16[:, sl]) + _mm_nt(q_in[:, sl], s_prev.astype(BF16))
        st[h] = s_prev * e_last[:, sl] + _mm_tn(v16[:, sl], k_out[:, sl])
        outs.append(o * lax.rsqrt(jnp.mean(o * o, axis=-1, keepdims=True) + NORM_EPS) * nw_ref[...])
    o_ref[...] = (jnp.concatenate(outs, axis=1) * _silu(g_ref[...].astype(F32))).astype(o_ref.dtype)


def _hgrn(pa, pb, layer, hgrn_lb, norm_w):
    nb, seq, _ = pa.shape
    spec = lambda col: pl.BlockSpec((None, CHUNK, BRANCH_WIDTH), lambda b, c: (b, c, col))
    lb = hgrn_lb.astype(F32)
    nw = norm_w.reshape(1, HGRN_DK).astype(F32)
    return pl.pallas_call(
        functools.partial(_hgrn_kernel, layer),
        grid=(nb, seq // CHUNK),
        in_specs=[spec(PA_COLS["hq"]), spec(PB_HF_COL), spec(PA_COLS["hi"]), spec(PA_COLS["hg"]),
                  pl.BlockSpec(lb.shape, lambda b, c: (0, 0)), pl.BlockSpec(nw.shape, lambda b, c: (0, 0))],
        out_specs=spec(0),
        out_shape=jax.ShapeDtypeStruct((nb, seq, BRANCH_WIDTH), BF16),
        scratch_shapes=[pltpu.VMEM((HGRN_HEADS, HGRN_DK, HGRN_DK), F32)],
        compiler_params=_cparams("parallel", "arbitrary"),
        name="hgrn2",
    )(pa, pb, pa, pa, lb, nw)


def _merge_kernel(x_ref, mod_ref, o0_ref, o1_ref, o2_ref, o3_ref, wm_ref, wb_ref, wo_ref, g_ref, b_ref, out_ref):
    m = mod_ref[...]
    x = x_ref[...]
    h = (x * (1.0 + m[1:2]) + m[0:1]).astype(BF16)
    y = None
    for i, o_ref in enumerate((o0_ref, o1_ref, o2_ref, o3_ref)):
        gate = jax.nn.sigmoid(_mm(h, wm_ref[:, i * D_MODEL:(i + 1) * D_MODEL]))
        term = gate * _mm(o_ref[...], wb_ref[i])
        y = term if y is None else y + term
    mix = _mm(y.astype(BF16), wo_ref[...])
    out_ref[...] = _layer_norm_rows(DEEPNORM_ALPHA * x + m[2:3] * mix, g_ref[...], b_ref[...])


def _merge(x2, mod, branches, w_merge, w_branch, w_o, ln_g, ln_b, seq):
    t = x2.shape[0]
    tm = 256
    per_b = seq // tm
    rows = pl.BlockSpec((tm, D_MODEL), lambda i: (i, 0))
    brow = pl.BlockSpec((tm, BRANCH_WIDTH), lambda i: (i, 0))
    return pl.pallas_call(
        _merge_kernel,
        grid=(t // tm,),
        in_specs=[rows, pl.BlockSpec((None, 6, D_MODEL), lambda i: (i // per_b, 0, 0))] + [brow] * N_BRANCH
        + [_const_spec(w_merge.shape), _const_spec(w_branch.shape), _const_spec(w_o.shape),
           _const_spec((1, D_MODEL)), _const_spec((1, D_MODEL))],
        out_specs=rows,
        out_shape=jax.ShapeDtypeStruct((t, D_MODEL), F32),
        compiler_params=_cparams("parallel"),
        name="merge_out_ln",
    )(x2, mod, *branches, w_merge, w_branch, w_o, ln_g.reshape(1, -1), ln_b.reshape(1, -1))


def _ffn_kernel(x_ref, mod_ref, wu_ref, cw_ref, cb_ref, wd_ref, g_ref, b_ref, out_ref, ubuf0, ubuf1, tail, act):
    tm = x_ref.shape[0]
    halo = 8
    assert N_FF_CHUNKS % 2 == 1

    @pl.when(pl.program_id(1) == 0)
    def _():
        tail[...] = jnp.zeros_like(tail)

    m = mod_ref[...]
    x = x_ref[...]
    h = (x * (1.0 + m[4:5]) + m[3:4]).astype(BF16)

    def up(j, ubuf):
        ubuf[0:halo, :] = tail[j]
        ubuf[halo:halo + tm, :] = _mm(h, wu_ref[j])
        tail[j] = ubuf[tm:tm + halo, :]

    def gate(j, ubuf):
        cw = cw_ref[j]
        u = jnp.broadcast_to(cb_ref[j], (tm, 2 * FF_CHUNK))
        for k in range(FFN_CONV):
            off = halo - (FFN_CONV - 1) + k
            u = u + cw[k:k + 1, :] * ubuf[off:off + tm, :]
        act[j] = (_silu(u[:, :FF_CHUNK]) * u[:, FF_CHUNK:]).astype(BF16)

    bufs = (ubuf0, ubuf1)
    up(0, ubuf0)
    for j in range(N_FF_CHUNKS):
        if j + 1 < N_FF_CHUNKS:
            up(j + 1, bufs[(j + 1) % 2])
        gate(j, bufs[j % 2])
    ffn = _mm(act[0], wd_ref[0])
    for j in range(1, N_FF_CHUNKS):
        ffn = ffn + _mm(act[j], wd_ref[j])
    out_ref[...] = _layer_norm_rows(DEEPNORM_ALPHA * x + m[5:6] * ffn, g_ref[...], b_ref[...])


def _ffn(x2, mod, w_up, conv_w, conv_b, w_down, ln_g, ln_b, nb, seq):
    tm = 512
    per_b = seq // tm
    rows = pl.BlockSpec((tm, D_MODEL), lambda b, i: (b * per_b + i, 0))
    return pl.pallas_call(
        _ffn_kernel,
        grid=(nb, per_b),
        in_specs=[rows, pl.BlockSpec((None, 6, D_MODEL), lambda b, i: (b, 0, 0)),
                  _const_spec(w_up.shape), _const_spec(conv_w.shape), _const_spec(conv_b.shape),
                  _const_spec(w_down.shape), _const_spec((1, D_MODEL)), _const_spec((1, D_MODEL))],
        out_specs=rows,
        out_shape=jax.ShapeDtypeStruct((nb * seq, D_MODEL), F32),
        scratch_shapes=[pltpu.VMEM((tm + 8, 2 * FF_CHUNK), F32),
                        pltpu.VMEM((tm + 8, 2 * FF_CHUNK), F32),
                        pltpu.VMEM((N_FF_CHUNKS, 8, 2 * FF_CHUNK), F32),
                        pltpu.VMEM((N_FF_CHUNKS, tm, FF_CHUNK), BF16)],
        compiler_params=_cparams("parallel", "arbitrary"),
        name="conv_ffn_ln",
    )(x2, mod, w_up, conv_w, conv_b, w_down, ln_g.reshape(1, -1), ln_b.reshape(1, -1))


def _pack_in_proj(w_in):
    offs = np.concatenate([[0], np.cumsum(SECTION_SIZES)])
    sec = {n: w_in[:, offs[i]:offs[i + 1]] for i, n in enumerate(SECTION_NAMES)}
    w = BRANCH_WIDTH
    sec.update({f"q{g}": sec["dq"][:, g * w:(g + 1) * w] for g in range(DIL_GROUPS)}, k=sec["dk"], v=sec["dv"])
    wa = jnp.concatenate([sec[n] for n in sorted(PA_COLS, key=PA_COLS.get)], axis=1)
    wd = jnp.concatenate([sec[n] for n in sorted(PD_COLS, key=PD_COLS.get)], axis=1)
    dt_pad = jnp.zeros((D_MODEL, 2 * LANES - SSD_HEADS), w_in.dtype)
    wb = jnp.concatenate([sec["hf"], sec["dt"], dt_pad, sec["xbc"]], axis=1)
    return wa.astype(BF16), wd.astype(BF16), wb.astype(BF16)


def _chunk_gate_value(a):
    lead = a.shape[:-1]
    a = a.reshape(lead + (2, N_FF_CHUNKS, FF_CHUNK))
    a = jnp.moveaxis(a, -2, 0)
    return a.reshape((N_FF_CHUNKS,) + lead + (2 * FF_CHUNK,))


def kernel(x, c, w_ada, b_ada, w_in, w_merge, ssd_conv_w, ssd_conv_b, ssd_dt_bias, ssd_a_log, ssd_d, ssd_norm_w, hgrn_lb, hgrn_norm_w, w_branch_out, w_o, ln1_g, ln1_b, w_up, ffn_conv_w, ffn_conv_b, w_down, ln2_g, ln2_b):
    nb, seq, _ = x.shape
    assert seq % (DIL_PATTERNS[-1][1] * DIL_QBLOCKS * DIL_BLOCK) == 0 and x.shape[-1] == D_MODEL
    mods = _modulation(c, w_ada, b_ada).reshape(DEPTH, nb, 6, D_MODEL)
    x2 = x.reshape(nb * seq, D_MODEL)
    for layer in range(DEPTH):
        mod = mods[layer]
        pa, *pds, pb = (a.reshape(nb, seq, -1) for a in _project(x2, mod, *_pack_in_proj(w_in[layer]), seq))
        o_ssd = _ssd(pa, pb, ssd_conv_w[layer], ssd_conv_b[layer], ssd_dt_bias[layer], ssd_a_log[layer],
                     ssd_d[layer], ssd_norm_w[layer])
        o_dil = _dilated(pds)
        o_ret = _retention(pa)
        o_hgrn = _hgrn(pa, pb, layer, hgrn_lb, hgrn_norm_w[layer])
        flat = lambda a: a.reshape(nb * seq, BRANCH_WIDTH)
        x2 = _merge(x2, mod, (flat(o_ssd), o_dil, flat(o_ret), flat(o_hgrn)), w_merge[layer].astype(BF16),
                    w_branch_out[layer].astype(BF16), w_o[layer].astype(BF16), ln1_g[layer], ln1_b[layer], seq)
        x2 = _ffn(x2, mod, _chunk_gate_value(w_up[layer]).astype(BF16), _chunk_gate_value(ffn_conv_w[layer]),
                  _chunk_gate_value(ffn_conv_b[layer].reshape(1, -1)), w_down[layer].reshape(N_FF_CHUNKS, FF_CHUNK, D_MODEL).astype(BF16),
                  ln2_g[layer], ln2_b[layer], nb, seq)
    return x2.reshape(nb, seq, D_MODEL)
```

```python
import functools
import math

import numpy as np
import jax
import jax.numpy as jnp
from jax import lax
from jax.experimental import pallas as pl
from jax.experimental.pallas import tpu as pltpu

F32 = jnp.float32
BF16 = jnp.bfloat16

D_MODEL = 1024
DEPTH = 2
DEEPNORM_ALPHA = (2 * DEPTH) ** 0.25
NORM_EPS = 1e-5

N_BRANCH = 4
BRANCH_WIDTH = D_MODEL // 2

SSD_HEAD_DIM = 64
SSD_HEADS = BRANCH_WIDTH // SSD_HEAD_DIM
SSD_GROUPS = 2
SSD_STATE = 64
SSD_CONV = 4
SSD_CONV_DIM = BRANCH_WIDTH + 2 * SSD_GROUPS * SSD_STATE

DIL_HEAD_DIM = 64
DIL_HEADS = BRANCH_WIDTH // DIL_HEAD_DIM
DIL_PATTERNS = ((128, 1), (512, 4), (2048, 16))
DIL_GROUPS = len(DIL_PATTERNS)
DIL_BLOCK = 128

RET_HEADS = 4
RET_DK = BRANCH_WIDTH // RET_HEADS

HGRN_HEADS = 4
HGRN_DK = BRANCH_WIDTH // HGRN_HEADS
HGRN_SUB = 4

D_FF = 256 * ((8 * D_MODEL // 3 + 255) // 256)
FFN_CONV = 3
FF_CHUNK = 256
N_FF_CHUNKS = D_FF // FF_CHUNK

SECTION_SIZES = (
    BRANCH_WIDTH, SSD_CONV_DIM, SSD_HEADS,
    DIL_GROUPS * BRANCH_WIDTH, BRANCH_WIDTH, BRANCH_WIDTH,
    BRANCH_WIDTH, BRANCH_WIDTH, BRANCH_WIDTH, BRANCH_WIDTH,
    BRANCH_WIDTH, BRANCH_WIDTH, BRANCH_WIDTH, BRANCH_WIDTH,
)
SECTION_NAMES = ("z", "xbc", "dt", "dq", "dk", "dv", "rq", "rk", "rv", "rg", "hq", "hf", "hi", "hg")

LANES = 128
CHUNK = 128
PD_COLS = {n: i for i, n in enumerate(("q0", "q1", "k", "v", "q2"))}
VMEM_LIMIT = 56 * 1024 * 1024


MIXER_ROWS = 2


def _rows_per_step(nb):
    return MIXER_ROWS if nb % MIXER_ROWS == 0 else 1


def _cparams(*sem):
    return pltpu.CompilerParams(dimension_semantics=sem, vmem_limit_bytes=VMEM_LIMIT)


def _silu(v):
    return v * jax.nn.sigmoid(v)


def _mm(a, b):
    return jnp.dot(a, b, preferred_element_type=F32)


def _mm_nt(a, b):
    return lax.dot_general(a, b, (((1,), (1,)), ((), ())), preferred_element_type=F32)


def _mm_tn(a, b):
    return lax.dot_general(a, b, (((0,), (0,)), ((), ())), preferred_element_type=F32)


def _split2(a):
    hi = a.astype(BF16)
    lo = (a - hi.astype(F32)).astype(BF16)
    return hi, lo


def _split3(a):
    hi = a.astype(BF16)
    r = a - hi.astype(F32)
    mid = r.astype(BF16)
    lo = (r - mid.astype(F32)).astype(BF16)
    return hi, mid, lo


def _mm_f32(a, b):
    ah, al = _split2(a)
    bh, bl = _split2(b)
    return _mm(ah, bh) + (_mm(ah, bl) + _mm(al, bh))


def _cumsum_rows(tri, a):
    hi, mid, lo = _split3(a)
    return _mm(jnp.concatenate([tri, tri], axis=1), jnp.concatenate([hi, mid], axis=0)) + _mm(tri, lo)


def _expand(a, e):
    hi, lo = _split2(a)
    return _mm(jnp.concatenate([hi, lo], axis=1), jnp.concatenate([e, e], axis=0))


def _iota(shape, dim):
    return lax.broadcasted_iota(jnp.int32, shape, dim)


def _tri(n):
    return jnp.where(_iota((n, n), 1) <= _iota((n, n), 0), 1.0, 0.0).astype(BF16)


def _head_expander(width):
    shape = (LANES, BRANCH_WIDTH)
    shift = int(math.log2(width))
    return jnp.where(jnp.right_shift(_iota(shape, 1), shift) == _iota(shape, 0), 1.0, 0.0).astype(BF16)


def _layer_norm_rows(v, g, b):
    vc = v - jnp.mean(v, axis=-1, keepdims=True)
    return vc * lax.rsqrt(jnp.mean(vc * vc, axis=-1, keepdims=True) + NORM_EPS) * g + b


def _mod_kernel(c_ref, w_ref, b_ref, o_ref):
    o_ref[0] = _mm_f32(_silu(c_ref[...]), w_ref[0]) + b_ref[0]


def _modulation(c, w_ada, b_ada):
    nb = c.shape[0]
    tn = 1536
    return pl.pallas_call(
        _mod_kernel,
        grid=(DEPTH, 6 * D_MODEL // tn),
        in_specs=[pl.BlockSpec((nb, D_MODEL), lambda l, j: (0, 0)),
                  pl.BlockSpec((1, D_MODEL, tn), lambda l, j: (l, 0, j)),
                  pl.BlockSpec((1, 1, tn), lambda l, j: (l, 0, j))],
        out_specs=pl.BlockSpec((1, nb, tn), lambda l, j: (l, 0, j)),
        out_shape=jax.ShapeDtypeStruct((DEPTH, nb, 6 * D_MODEL), F32),
        compiler_params=_cparams("parallel", "parallel"),
        name="adaln_mod",
    )(c, w_ada, b_ada.reshape(DEPTH, 1, 6 * D_MODEL))


def _const_spec(shape):
    zeros = (0,) * len(shape)
    return pl.BlockSpec(shape, lambda *_: zeros, pipeline_mode=pl.Buffered(1))


def _destride_matrix(n, r):
    per = n // r
    i = _iota((n, n), 0)
    src = jnp.left_shift(jnp.bitwise_and(i, per - 1), int(math.log2(r))) + jnp.right_shift(i, int(math.log2(per)))
    return jnp.where(_iota((n, n), 1) == src, 1.0, 0.0).astype(BF16)


def _proj_kernel(x_ref, mod_ref, wd_ref, od0_ref, od1_ref, od2_ref):
    m = mod_ref[...]
    h = (x_ref[...] * (1.0 + m[1:2]) + m[0:1]).astype(BF16)
    d = _mm(h, wd_ref[...]).astype(BF16)
    w = BRANCH_WIDTH
    tm = d.shape[0]
    od0_ref[...] = jnp.concatenate([d[:, 0:w], d[:, 2 * w:4 * w]], axis=1)
    od1_ref[...] = _mm(_destride_matrix(tm, DIL_PATTERNS[1][1]), d[:, w:4 * w]).astype(BF16)
    od2_ref[...] = _mm(_destride_matrix(tm, DIL_PATTERNS[2][1]), d[:, 2 * w:5 * w]).astype(BF16)


PROJ_ROWS = 256
DIL_OPERAND_COLS = ((0, 1, 2), (0, 1, 2), (2, 0, 1))


def _project_dilated(x2, mod, wd, seq):
    t = x2.shape[0]
    tm = PROJ_ROWS
    per_b = seq // tm
    qkv = 3 * BRANCH_WIDTH
    return pl.pallas_call(
        _proj_kernel,
        grid=(t // tm,),
        in_specs=[pl.BlockSpec((tm, D_MODEL), lambda i: (i, 0)),
                  pl.BlockSpec((None, 6, D_MODEL), lambda i: (i // per_b, 0, 0)),
                  _const_spec(wd.shape)],
        out_specs=[pl.BlockSpec((tm, qkv), lambda i: (i, 0))] * DIL_GROUPS,
        out_shape=[jax.ShapeDtypeStruct((t, qkv), BF16)] * DIL_GROUPS,
        compiler_params=_cparams("parallel"),
        name="in_proj_dilated",
    )(x2, mod, wd)


SSD_HALO = 8


def _ssd_init(xbuf, st):
    xbuf[:, 0:SSD_HALO, :] = jnp.zeros((xbuf.shape[0], SSD_HALO, SSD_CONV_DIM), F32)
    st[...] = jnp.zeros_like(st)


def _ssd_chunk(p_ref, o_ref, half, cw_ref, cb_ref, dtb_ref, alog_ref, dsk_ref, nw_ref, xbuf, st):
    c = CHUNK
    halo = SSD_HALO
    xbc0 = BRANCH_WIDTH
    dt0 = BRANCH_WIDTH + SSD_CONV_DIM
    for b in range(p_ref.shape[0]):
        xbuf[b, halo:halo + c, :] = p_ref[b, :, xbc0:dt0]
        acc = jnp.broadcast_to(cb_ref[...], (c, SSD_CONV_DIM))
        for k in range(SSD_CONV):
            off = halo - (SSD_CONV - 1) + k
            acc = acc + cw_ref[k:k + 1, :] * xbuf[b, off:off + c, :]
        xbuf[b, 0:halo, :] = xbuf[b, c:c + halo, :]
        y = _silu(acc)
        xs = y[:, :BRANCH_WIDTH]
        bm = y[:, BRANCH_WIDTH:BRANCH_WIDTH + LANES]
        cm = y[:, BRANCH_WIDTH + LANES:]

        dtr = p_ref[b, :, dt0:dt0 + LANES] + dtb_ref[...]
        dt = jnp.maximum(dtr, 0.0) + jnp.log1p(jnp.exp(-jnp.abs(dtr)))
        da = dt * (-jnp.exp(alog_ref[...]))
        cs = _cumsum_rows(_tri(c), da)
        cs_t = cs.T
        tot = cs[c - 1:c, :]

        stack = jnp.concatenate([dt, jnp.exp(tot - cs), jnp.exp(cs), jnp.broadcast_to(jnp.exp(tot), (8, LANES))], axis=0)
        ex = _expand(stack, _head_expander(SSD_HEAD_DIM))
        dt_e, ds_e, ecs_e, dec_e = ex[0:c], ex[c:2 * c], ex[2 * c:3 * c], ex[3 * c:3 * c + 1]

        xdt = xs * dt_e
        xds = xdt * ds_e
        causal = _iota((c, c), 1) <= _iota((c, c), 0)
        lane = _iota((c, LANES), 1)
        bm16 = bm.astype(BF16)
        cbs = []
        for g in range(SSD_GROUPS):
            cm_g = jnp.where(jnp.right_shift(lane, 6) == g, cm, 0.0).astype(BF16)
            cbs.append(_mm_nt(cm_g, bm16))
        parts = []
        for p in range(SSD_HEADS // 2):
            g = (2 * p) // (SSD_HEADS // SSD_GROUPS)
            ms = []
            for e in range(2):
                h = 2 * p + e
                diff = cs[:, h:h + 1] - cs_t[h:h + 1, :]
                ms.append((cbs[g] * jnp.exp(jnp.where(causal, diff, -jnp.inf))).astype(BF16))
            xp = xdt[:, p * LANES:(p + 1) * LANES]
            xbd = jnp.concatenate([jnp.where(lane < SSD_HEAD_DIM, xp, 0.0), jnp.where(lane >= SSD_HEAD_DIM, xp, 0.0)],
                                  axis=0).astype(BF16)
            parts.append(_mm(jnp.concatenate(ms, axis=1), xbd))
        y_diag = jnp.concatenate(parts, axis=1)

        s_prev = st[b]
        y_off = _mm(cm.astype(BF16), s_prev.astype(BF16)) * ecs_e
        upd = _mm(bm.T.astype(BF16), xds.astype(BF16))
        shape = (LANES, BRANCH_WIDTH)
        same_group = jnp.right_shift(_iota(shape, 0), 6) == jnp.right_shift(_iota(shape, 1), 8)
        st[b] = s_prev * dec_e + jnp.where(same_group, upd, 0.0)

        yv = (y_diag + y_off + xs * dsk_ref[...]) * _silu(p_ref[b, :, 0:BRANCH_WIDTH])
        gw = BRANCH_WIDTH // SSD_GROUPS
        outs = []
        for g in range(SSD_GROUPS):
            yg = yv[:, g * gw:(g + 1) * gw]
            outs.append(yg * lax.rsqrt(jnp.mean(yg * yg, axis=-1, keepdims=True) + NORM_EPS))
        o_ref[b, half * c:(half + 1) * c, :] = (jnp.concatenate(outs, axis=1) * nw_ref[...]).astype(o_ref.dtype)


def _ssd(x, mod, w, conv_w, conv_b, dt_bias, a_log, d_skip, norm_w):
    pad = LANES - SSD_HEADS
    row = lambda v: v.reshape(1, -1).astype(F32)
    aux = (conv_w.astype(F32), row(conv_b), row(jnp.pad(dt_bias, (0, pad))), row(jnp.pad(a_log, (0, pad))),
           row(jnp.repeat(d_skip, SSD_HEAD_DIM)), row(norm_w))
    return _fused_mixer("ssd_mixer", _ssd_init, _ssd_chunk, x, mod, w, aux,
                        [(CHUNK + SSD_HALO, SSD_CONV_DIM), (LANES, BRANCH_WIDTH)])


def _alibi_slopes(n):
    def pow2(k):
        start = 2.0 ** (-8.0 / k)
        return [start ** (i + 1) for i in range(k)]
    if math.log2(n).is_integer():
        s = pow2(n)
    else:
        c = 2 ** math.floor(math.log2(n))
        s = pow2(c) + pow2(2 * c)[0::2][: n - c]
    return [float(np.float32(v)) for v in s]


def _dil_kernel(dilation, n_back, slopes, q_ref, kc_ref, kp_ref, vc_ref, vp_ref, o_ref, lse_ref):
    blk = DIL_BLOCK
    rho = pl.program_id(2)
    log2e = 1.0 / math.log(2.0)

    def rows_of(ref, start, sl):
        if len(ref.shape) == 2:
            return ref[start:start + blk, sl]
        per = ref.shape[1]
        return jnp.concatenate([ref[t, :, sl] for t in range(start // per, (start + blk) // per)], axis=0)

    qi = _iota((blk, 2 * blk), 0)
    kj = _iota((blk, 2 * blk), 1)
    dist = qi - kj + blk
    in_window = (dist >= 0) & (dist <= n_back)
    neg_dist2 = (dist * dilation).astype(F32) * (-log2e)
    lane = _iota((blk, LANES), 1)
    for sb in range(DIL_QBLOCKS):
        if sb == 0:
            valid = in_window & (kj >= jnp.where(pl.program_id(1) > 0, 0, blk))
        else:
            valid = in_window
        bias = jnp.where(valid, neg_dist2, -jnp.inf)
        rows = pl.ds(rho + sb * blk * dilation, blk, stride=dilation) if dilation > 1 else pl.ds(sb * blk, blk)
        lse_tile = jnp.zeros((blk, LANES), F32)
        for p in range(DIL_HEADS // 2):
            sl = slice(p * LANES, (p + 1) * LANES)
            qp = rows_of(q_ref, sb * blk, sl).astype(F32) * (DIL_HEAD_DIM ** -0.5 * log2e)
            if sb == 0:
                kk = jnp.concatenate([rows_of(kp_ref, 0, sl), rows_of(kc_ref, 0, sl)], axis=0)
                vv = jnp.concatenate([rows_of(vp_ref, 0, sl), rows_of(vc_ref, 0, sl)], axis=0)
            else:
                kk = jnp.concatenate([rows_of(kc_ref, (sb - 1) * blk, sl), rows_of(kc_ref, sb * blk, sl)], axis=0)
                vv = jnp.concatenate([rows_of(vc_ref, (sb - 1) * blk, sl), rows_of(vc_ref, sb * blk, sl)], axis=0)
            pair = []
            for e in range(2):
                h = 2 * p + e
                own = (lane < DIL_HEAD_DIM) if e == 0 else (lane >= DIL_HEAD_DIM)
                qm = jnp.where(own, qp, 0.0).astype(BF16)
                s = _mm_nt(qm, kk) + slopes[h] * bias
                m = jnp.max(s, axis=-1, keepdims=True)
                pexp = jnp.exp2(s - m)
                l = jnp.sum(pexp, axis=-1, keepdims=True)
                pair.append(_mm(pexp.astype(BF16), vv) / l)
                lse_tile = jnp.where(lane == h, (m + jnp.log2(l)) * math.log(2.0), lse_tile)
            o_ref[p, rows, :] = jnp.where(lane < DIL_HEAD_DIM, pair[0], pair[1])
        lse_ref[rows, :] = lse_tile


DIL_QBLOCKS = 2


def _dilated_group(pd, g):
    nb, seq, _ = pd.shape
    window, r = DIL_PATTERNS[g]
    n = seq // r
    qrows = DIL_QBLOCKS * DIL_BLOCK
    w = BRANCH_WIDTH
    qc, kc, vc = DIL_OPERAND_COLS[g]
    if r == 1:
        view = pd
        cur = lambda col: pl.BlockSpec((None, qrows, w), lambda b, i, rho: (b, i, col))
        prev = lambda col: pl.BlockSpec((None, DIL_BLOCK, w),
                                        lambda b, i, rho: (b, jnp.maximum(DIL_QBLOCKS * i - 1, 0), col))
    else:
        per = PROJ_ROWS // r
        view = pd.reshape(nb, seq // PROJ_ROWS, r, per, 3 * w)
        cur = lambda col: pl.BlockSpec((None, qrows // per, None, per, w), lambda b, i, rho: (b, i, rho, 0, col))
        prev = lambda col: pl.BlockSpec((None, DIL_BLOCK // per, None, per, w),
                                        lambda b, i, rho: (b, jnp.maximum(DIL_QBLOCKS * i - 1, 0), rho, 0, col))
    slopes = _alibi_slopes(DIL_GROUPS * DIL_HEADS)[g * DIL_HEADS:(g + 1) * DIL_HEADS]
    pairs = DIL_HEADS // 2
    return pl.pallas_call(
        functools.partial(_dil_kernel, r, window // r, slopes),
        grid=(nb, n // qrows, r),
        in_specs=[cur(qc), cur(kc), prev(kc), cur(vc), prev(vc)],
        out_specs=[pl.BlockSpec((pairs, None, qrows * r, LANES), lambda b, i, rho: (0, b, i, 0)),
                   pl.BlockSpec((None, qrows * r, LANES), lambda b, i, rho: (b, i, 0))],
        out_shape=[jax.ShapeDtypeStruct((pairs, nb, seq, LANES), F32),
                   jax.ShapeDtypeStruct((nb, seq, LANES), F32)],
        compiler_params=_cparams("parallel", "parallel", "arbitrary"),
        name=f"dilated_attn_g{g}",
    )(view, view, view, view, view)


def _dil_combine_kernel(l0_ref, l1_ref, l2_ref, o0_ref, o1_ref, o2_ref, out_ref):
    ls = [l0_ref[...], l1_ref[...], l2_ref[...]]
    m = jnp.maximum(jnp.maximum(ls[0], ls[1]), ls[2])
    es = [jnp.exp(l - m) for l in ls]
    den = es[0] + es[1] + es[2]
    e = _head_expander(DIL_HEAD_DIM)
    acc = None
    for ev, o_ref in zip(es, (o0_ref, o1_ref, o2_ref)):
        o = jnp.concatenate([o_ref[p] for p in range(DIL_HEADS // 2)], axis=1)
        term = _expand(ev / den, e) * o
        acc = term if acc is None else acc + term
    out_ref[...] = acc.astype(out_ref.dtype)


def _dilated(pds):
    nb, seq, _ = pds[0].shape
    res = [_dilated_group(pd, g) for g, pd in enumerate(pds)]
    t = nb * seq
    tm = 512
    pairs = DIL_HEADS // 2
    lspec = pl.BlockSpec((tm, LANES), lambda i: (i, 0))
    ospec = pl.BlockSpec((pairs, tm, LANES), lambda i: (0, i, 0))
    return pl.pallas_call(
        _dil_combine_kernel,
        grid=(t // tm,),
        in_specs=[lspec] * 3 + [ospec] * 3,
        out_specs=pl.BlockSpec((tm, BRANCH_WIDTH), lambda i: (i, 0)),
        out_shape=jax.ShapeDtypeStruct((t, BRANCH_WIDTH), BF16),
        compiler_params=_cparams("parallel"),
        name="dilated_combine",
    )(*[r[1].reshape(t, LANES) for r in res], *[r[0].reshape(pairs, t, LANES) for r in res])


def _fused_mixer_kernel(init_fn, chunk_fn, n_aux, x0_ref, xa_ref, xb_ref, mod_ref, w_ref, *rest):
    aux, o_ref, p0, p1, scratch = rest[:n_aux], rest[n_aux], rest[n_aux + 1], rest[n_aux + 2], rest[n_aux + 3:]
    rows = x0_ref.shape[0]
    m = mod_ref[...]

    def project(x_ref, p_ref):
        h = jnp.concatenate([(x_ref[b] * (1.0 + m[b, 1:2]) + m[b, 0:1]).astype(BF16) for b in range(rows)], axis=0)
        res = _mm(h, w_ref[...])
        for b in range(rows):
            p_ref[b] = res[b * CHUNK:(b + 1) * CHUNK]

    @pl.when(pl.program_id(1) == 0)
    def _():
        init_fn(*scratch)
        project(x0_ref, p0)

    chunk_fn(p0, o_ref, 0, *aux, *scratch)
    project(xa_ref, p1)
    chunk_fn(p1, o_ref, 1, *aux, *scratch)
    project(xb_ref, p0)


def _fused_mixer(name, init_fn, chunk_fn, x, mod, w, aux, scratch_shapes):
    nb, seq, _ = x.shape
    rows = _rows_per_step(nb)
    nc = seq // CHUNK
    n = w.shape[1]
    xspec = lambda chunk_of: pl.BlockSpec((rows, CHUNK, D_MODEL), lambda b, s: (b, chunk_of(s), 0))
    return pl.pallas_call(
        functools.partial(_fused_mixer_kernel, init_fn, chunk_fn, len(aux)),
        grid=(nb // rows, nc // 2),
        in_specs=[xspec(lambda s: 0), xspec(lambda s: 2 * s + 1), xspec(lambda s: jnp.minimum(2 * s + 2, nc - 1)),
                  pl.BlockSpec((rows, 6, D_MODEL), lambda b, s: (b, 0, 0)), _const_spec(w.shape)]
        + [_const_spec(a.shape) for a in aux],
        out_specs=pl.BlockSpec((rows, 2 * CHUNK, BRANCH_WIDTH), lambda b, s: (b, s, 0)),
        out_shape=jax.ShapeDtypeStruct((nb, seq, BRANCH_WIDTH), BF16),
        scratch_shapes=[pltpu.VMEM((rows, CHUNK, n), F32), pltpu.VMEM((rows, CHUNK, n), F32)]
        + [pltpu.VMEM((rows,) + s, F32) for s in scratch_shapes],
        compiler_params=_cparams("parallel", "arbitrary"),
        name=name,
    )(x, x, x, mod, w, *aux)


def _ret_init(st):
    st[...] = jnp.zeros_like(st)


def _ret_chunk(p_ref, o_ref, half, st):
    c = CHUNK
    w = BRANCH_WIDTH
    scale = RET_DK ** -0.5
    row = _iota((c, c), 0)
    col = _iota((c, c), 1)
    rel = (row - col).astype(F32)
    pos = row.astype(F32)
    for b in range(p_ref.shape[0]):
        outs = []
        for h in range(RET_HEADS):
            lg = math.log(1.0 - 2.0 ** (-5.0 - h))
            decay = jnp.where(row >= col, jnp.exp(lg * jnp.maximum(rel, 0.0)), 0.0) * scale
            qf, kf = (p_ref[b, :, i * w + h * RET_DK:i * w + (h + 1) * RET_DK] for i in range(2))
            vh = p_ref[b, :, 2 * w + h * RET_DK:2 * w + (h + 1) * RET_DK].astype(BF16)
            inner = _mm((_mm_nt(qf.astype(BF16), kf.astype(BF16)) * decay).astype(BF16), vh)
            s_prev = st[b, h]
            q_dec = (qf * jnp.exp(lg * (pos + 1.0))).astype(BF16)
            o = inner + _mm(q_dec, s_prev.astype(BF16))
            k_dec = (kf * (jnp.exp(lg * (c - 1.0 - pos)) * scale)).astype(BF16)
            st[b, h] = s_prev * math.exp(lg * c) + _mm_tn(k_dec, vh)
            oc = o - jnp.mean(o, axis=-1, keepdims=True)
            outs.append(oc * lax.rsqrt(jnp.mean(oc * oc, axis=-1, keepdims=True) + NORM_EPS))
        gate = _silu(p_ref[b, :, 3 * w:4 * w])
        o_ref[b, half * c:(half + 1) * c, :] = (jnp.concatenate(outs, axis=1) * gate).astype(o_ref.dtype)


def _retention(x, mod, w):
    return _fused_mixer("retention", _ret_init, _ret_chunk, x, mod, w, (), [(RET_HEADS, RET_DK, RET_DK)])


def _hgrn_init(st):
    st[...] = jnp.zeros_like(st)


def _hgrn_chunk(layer, p_ref, o_ref, half, lb_ref, nw_ref, st):
    c = CHUNK
    width = BRANCH_WIDTH
    neg_inf = -jnp.inf

    rows = [lb_ref[l:l + 1, :] for l in range(DEPTH)]
    mx = functools.reduce(jnp.maximum, rows)
    es = [jnp.exp(r - mx) for r in rows]
    den = functools.reduce(lambda a, b: a + b, es)
    sm = [e / den for e in es]
    lb = functools.reduce(lambda a, b: a + b, sm[:layer + 1]) - sm[0]

    ti = _iota((c, c), 0)
    tj = _iota((c, c), 1)
    le_t = jnp.where(tj <= ti, 1.0, 0.0)
    sizes = [c >> k for k in range(int(math.log2(c // HGRN_SUB)))]
    mats = [le_t]
    for size in sizes:
        sh = int(math.log2(size))
        last_lower = jnp.left_shift(jnp.right_shift(ti, sh), sh) + (size // 2 - 1)
        mats.append(le_t - jnp.where(tj <= last_lower, 1.0, 0.0))
    seg = jnp.concatenate(mats, axis=0).astype(BF16)
    rowi = _iota((c, width), 0)
    sub = jnp.bitwise_and(rowi, HGRN_SUB - 1)
    heads = [slice(h * HGRN_DK, (h + 1) * HGRN_DK) for h in range(HGRN_HEADS)]

    for b in range(p_ref.shape[0]):
        forget = lb + (1.0 - lb) * jax.nn.sigmoid(p_ref[b, :, width:2 * width])
        lf = jnp.log(forget)
        kk = 1.0 - forget
        q = _silu(p_ref[b, :, 0:width])
        vf = p_ref[b, :, 2 * width:3 * width]
        v16 = vf.astype(BF16)
        sums = _cumsum_rows(seg, lf)
        lam = sums[0:c]

        o_acc = [jnp.zeros((c, HGRN_DK), F32) for _ in heads]
        for d in range(HGRN_SUB):
            if d == 0:
                prod = q * kk
                vd = vf
            else:
                ld = pltpu.roll(lam, d, 0)
                prod = q * pltpu.roll(kk, d, 0) * jnp.exp(jnp.where(sub >= d, lam - ld, neg_inf))
                vd = pltpu.roll(vf, d, 0)
            for h, sl in enumerate(heads):
                o_acc[h] = o_acc[h] + jnp.sum(prod[:, sl], axis=-1, keepdims=True) * vd[:, sl]

        attn = [jnp.zeros((c, c), F32) for _ in heads]
        for k, size in enumerate(sizes):
            mid = size // 2
            rel = sums[(k + 1) * c:(k + 2) * c]
            upper = jnp.bitwise_and(rowi, size - 1) >= mid
            z = (jnp.where(upper, q, kk) * jnp.exp(-jnp.abs(rel))).astype(BF16)
            sh = int(math.log2(size))
            pairs = ((jnp.right_shift(ti, sh) == jnp.right_shift(tj, sh))
                     & (jnp.bitwise_and(ti, size - 1) >= mid) & (jnp.bitwise_and(tj, size - 1) < mid))
            for h, sl in enumerate(heads):
                attn[h] = attn[h] + jnp.where(pairs, _mm_nt(z[:, sl], z[:, sl]), 0.0)

        lam_last = lam[c - 1:c, :]
        q_in = (q * jnp.exp(lam)).astype(BF16)
        k_out = (kk * jnp.exp(lam_last - lam)).astype(BF16)
        e_last = jnp.exp(lam_last)
        outs = []
        for h, sl in enumerate(heads):
            s_prev = st[b, h]
            o = o_acc[h] + _mm(attn[h].astype(BF16), v16[:, sl]) + _mm_nt(q_in[:, sl], s_prev.astype(BF16))
            st[b, h] = s_prev * e_last[:, sl] + _mm_tn(v16[:, sl], k_out[:, sl])
            outs.append(o * lax.rsqrt(jnp.mean(o * o, axis=-1, keepdims=True) + NORM_EPS) * nw_ref[...])
        gate = _silu(p_ref[b, :, 3 * width:4 * width])
        o_ref[b, half * c:(half + 1) * c, :] = (jnp.concatenate(outs, axis=1) * gate).astype(o_ref.dtype)


def _hgrn(x, mod, w, layer, hgrn_lb, norm_w):
    aux = (hgrn_lb.astype(F32), norm_w.reshape(1, HGRN_DK).astype(F32))
    return _fused_mixer("hgrn2", _hgrn_init, functools.partial(_hgrn_chunk, layer), x, mod, w, aux,
                        [(HGRN_HEADS, HGRN_DK, HGRN_DK)])


def _merge_kernel(x_ref, mod_ref, o0_ref, o1_ref, o2_ref, o3_ref, wm_ref, wb_ref, wo_ref, g_ref, b_ref, out_ref):
    m = mod_ref[...]
    x = x_ref[...]
    h = (x * (1.0 + m[1:2]) + m[0:1]).astype(BF16)
    y = None
    for i, o_ref in enumerate((o0_ref, o1_ref, o2_ref, o3_ref)):
        gate = jax.nn.sigmoid(_mm(h, wm_ref[:, i * D_MODEL:(i + 1) * D_MODEL]))
        term = gate * _mm(o_ref[...], wb_ref[i])
        y = term if y is None else y + term
    mix = _mm(y.astype(BF16), wo_ref[...])
    out_ref[...] = _layer_norm_rows(DEEPNORM_ALPHA * x + m[2:3] * mix, g_ref[...], b_ref[...])


def _merge(x2, mod, branches, w_merge, w_branch, w_o, ln_g, ln_b, seq):
    t = x2.shape[0]
    tm = 256
    per_b = seq // tm
    rows = pl.BlockSpec((tm, D_MODEL), lambda i: (i, 0))
    brow = pl.BlockSpec((tm, BRANCH_WIDTH), lambda i: (i, 0))
    return pl.pallas_call(
        _merge_kernel,
        grid=(t // tm,),
        in_specs=[rows, pl.BlockSpec((None, 6, D_MODEL), lambda i: (i // per_b, 0, 0))] + [brow] * N_BRANCH
        + [_const_spec(w_merge.shape), _const_spec(w_branch.shape), _const_spec(w_o.shape),
           _const_spec((1, D_MODEL)), _const_spec((1, D_MODEL))],
        out_specs=rows,
        out_shape=jax.ShapeDtypeStruct((t, D_MODEL), F32),
        compiler_params=_cparams("parallel"),
        name="merge_out_ln",
    )(x2, mod, *branches, w_merge, w_branch, w_o, ln_g.reshape(1, -1), ln_b.reshape(1, -1))


def _ffn_kernel(x_ref, mod_ref, wu_ref, cw_ref, cb_ref, wd_ref, g_ref, b_ref, out_ref, ubuf0, ubuf1, tail, act):
    tm = x_ref.shape[0]
    halo = 8
    assert N_FF_CHUNKS % 2 == 1

    @pl.when(pl.program_id(1) == 0)
    def _():
        tail[...] = jnp.zeros_like(tail)

    m = mod_ref[...]
    x = x_ref[...]
    h = (x * (1.0 + m[4:5]) + m[3:4]).astype(BF16)

    def up(j, ubuf):
        ubuf[0:halo, :] = tail[j]
        ubuf[halo:halo + tm, :] = _mm(h, wu_ref[j])
        tail[j] = ubuf[tm:tm + halo, :]

    def gate(j, ubuf):
        cw = cw_ref[j]
        u = jnp.broadcast_to(cb_ref[j], (tm, 2 * FF_CHUNK))
        for k in range(FFN_CONV):
            off = halo - (FFN_CONV - 1) + k
            u = u + cw[k:k + 1, :] * ubuf[off:off + tm, :]
        act[j] = (_silu(u[:, :FF_CHUNK]) * u[:, FF_CHUNK:]).astype(BF16)

    bufs = (ubuf0, ubuf1)
    up(0, ubuf0)
    for j in range(N_FF_CHUNKS):
        if j + 1 < N_FF_CHUNKS:
            up(j + 1, bufs[(j + 1) % 2])
        gate(j, bufs[j % 2])
    ffn = _mm(act[0], wd_ref[0])
    for j in range(1, N_FF_CHUNKS):
        ffn = ffn + _mm(act[j], wd_ref[j])
    out_ref[...] = _layer_norm_rows(DEEPNORM_ALPHA * x + m[5:6] * ffn, g_ref[...], b_ref[...])


def _ffn(x2, mod, w_up, conv_w, conv_b, w_down, ln_g, ln_b, nb, seq):
    tm = 512
    per_b = seq // tm
    rows = pl.BlockSpec((tm, D_MODEL), lambda b, i: (b * per_b + i, 0))
    return pl.pallas_call(
        _ffn_kernel,
        grid=(nb, per_b),
        in_specs=[rows, pl.BlockSpec((None, 6, D_MODEL), lambda b, i: (b, 0, 0)),
                  _const_spec(w_up.shape), _const_spec(conv_w.shape), _const_spec(conv_b.shape),
                  _const_spec(w_down.shape), _const_spec((1, D_MODEL)), _const_spec((1, D_MODEL))],
        out_specs=rows,
        out_shape=jax.ShapeDtypeStruct((nb * seq, D_MODEL), F32),
        scratch_shapes=[pltpu.VMEM((tm + 8, 2 * FF_CHUNK), F32),
                        pltpu.VMEM((tm + 8, 2 * FF_CHUNK), F32),
                        pltpu.VMEM((N_FF_CHUNKS, 8, 2 * FF_CHUNK), F32),
                        pltpu.VMEM((N_FF_CHUNKS, tm, FF_CHUNK), BF16)],
        compiler_params=_cparams("parallel", "arbitrary"),
        name="conv_ffn_ln",
    )(x2, mod, w_up, conv_w, conv_b, w_down, ln_g.reshape(1, -1), ln_b.reshape(1, -1))


def _pack_in_proj(w_in):
    offs = np.concatenate([[0], np.cumsum(SECTION_SIZES)])
    sec = {n: w_in[:, offs[i]:offs[i + 1]] for i, n in enumerate(SECTION_NAMES)}
    w = BRANCH_WIDTH
    sec.update({f"q{g}": sec["dq"][:, g * w:(g + 1) * w] for g in range(DIL_GROUPS)}, k=sec["dk"], v=sec["dv"])
    sec["dt_pad"] = jnp.zeros((D_MODEL, LANES - SSD_HEADS), w_in.dtype)
    cat = lambda names: jnp.concatenate([sec[n] for n in names], axis=1).astype(BF16)
    return {"dil": cat(sorted(PD_COLS, key=PD_COLS.get)), "ssd": cat(("z", "xbc", "dt", "dt_pad")),
            "ret": cat(("rq", "rk", "rv", "rg")), "hgrn": cat(("hq", "hf", "hi", "hg"))}


def _chunk_gate_value(a):
    lead = a.shape[:-1]
    a = a.reshape(lead + (2, N_FF_CHUNKS, FF_CHUNK))
    a = jnp.moveaxis(a, -2, 0)
    return a.reshape((N_FF_CHUNKS,) + lead + (2 * FF_CHUNK,))


def kernel(x, c, w_ada, b_ada, w_in, w_merge, ssd_conv_w, ssd_conv_b, ssd_dt_bias, ssd_a_log, ssd_d, ssd_norm_w, hgrn_lb, hgrn_norm_w, w_branch_out, w_o, ln1_g, ln1_b, w_up, ffn_conv_w, ffn_conv_b, w_down, ln2_g, ln2_b):
    nb, seq, _ = x.shape
    assert seq % (DIL_PATTERNS[-1][1] * DIL_QBLOCKS * DIL_BLOCK) == 0 and x.shape[-1] == D_MODEL
    mods = _modulation(c, w_ada, b_ada).reshape(DEPTH, nb, 6, D_MODEL)
    x2 = x.reshape(nb * seq, D_MODEL)
    for layer in range(DEPTH):
        mod = mods[layer]
        w_proj = _pack_in_proj(w_in[layer])
        x3 = x2.reshape(nb, seq, D_MODEL)
        o_dil = _dilated([a.reshape(nb, seq, -1) for a in _project_dilated(x2, mod, w_proj["dil"], seq)])
        o_ssd = _ssd(x3, mod, w_proj["ssd"], ssd_conv_w[layer], ssd_conv_b[layer], ssd_dt_bias[layer],
                     ssd_a_log[layer], ssd_d[layer], ssd_norm_w[layer])
        o_ret = _retention(x3, mod, w_proj["ret"])
        o_hgrn = _hgrn(x3, mod, w_proj["hgrn"], layer, hgrn_lb, hgrn_norm_w[layer])
        flat = lambda a: a.reshape(nb * seq, BRANCH_WIDTH)
        x2 = _merge(x2, mod, (flat(o_ssd), o_dil, flat(o_ret), flat(o_hgrn)), w_merge[layer].astype(BF16),
                    w_branch_out[layer].astype(BF16), w_o[layer].astype(BF16), ln1_g[layer], ln1_b[layer], seq)
        x2 = _ffn(x2, mod, _chunk_gate_value(w_up[layer]).astype(BF16), _chunk_gate_value(ffn_conv_w[layer]),
                  _chunk_gate_value(ffn_conv_b[layer].reshape(1, -1)), w_down[layer].reshape(N_FF_CHUNKS, FF_CHUNK, D_MODEL).astype(BF16),
                  ln2_g[layer], ln2_b[layer], nb, seq)
    return x2.reshape(nb, seq, D_MODEL)
```

```python
import functools
import math

import numpy as np
import jax
import jax.numpy as jnp
from jax import lax
from jax.experimental import pallas as pl
from jax.experimental.pallas import tpu as pltpu

F32 = jnp.float32
BF16 = jnp.bfloat16

D_MODEL = 1024
DEPTH = 2
DEEPNORM_ALPHA = (2 * DEPTH) ** 0.25
NORM_EPS = 1e-5

N_BRANCH = 4
BRANCH_WIDTH = D_MODEL // 2

SSD_HEAD_DIM = 64
SSD_HEADS = BRANCH_WIDTH // SSD_HEAD_DIM
SSD_GROUPS = 2
SSD_STATE = 64
SSD_CONV = 4
SSD_CONV_DIM = BRANCH_WIDTH + 2 * SSD_GROUPS * SSD_STATE

DIL_HEAD_DIM = 64
DIL_HEADS = BRANCH_WIDTH // DIL_HEAD_DIM
DIL_PATTERNS = ((128, 1), (512, 4), (2048, 16))
DIL_GROUPS = len(DIL_PATTERNS)
DIL_BLOCK = 128

RET_HEADS = 4
RET_DK = BRANCH_WIDTH // RET_HEADS

HGRN_HEADS = 4
HGRN_DK = BRANCH_WIDTH // HGRN_HEADS
HGRN_SUB = 4

D_FF = 256 * ((8 * D_MODEL // 3 + 255) // 256)
FFN_CONV = 3
FF_CHUNK = 256
N_FF_CHUNKS = D_FF // FF_CHUNK

SECTION_SIZES = (
    BRANCH_WIDTH, SSD_CONV_DIM, SSD_HEADS,
    DIL_GROUPS * BRANCH_WIDTH, BRANCH_WIDTH, BRANCH_WIDTH,
    BRANCH_WIDTH, BRANCH_WIDTH, BRANCH_WIDTH, BRANCH_WIDTH,
    BRANCH_WIDTH, BRANCH_WIDTH, BRANCH_WIDTH, BRANCH_WIDTH,
)
SECTION_NAMES = ("z", "xbc", "dt", "dq", "dk", "dv", "rq", "rk", "rv", "rg", "hq", "hf", "hi", "hg")

LANES = 128
CHUNK = 128
PD_COLS = {n: i for i, n in enumerate(("q0", "q1", "k", "v", "q2"))}
VMEM_LIMIT = 56 * 1024 * 1024


MIXER_ROWS = 2


def _rows_per_step(nb):
    return MIXER_ROWS if nb % MIXER_ROWS == 0 else 1


def _cparams(*sem):
    return pltpu.CompilerParams(dimension_semantics=sem, vmem_limit_bytes=VMEM_LIMIT)


def _silu(v):
    return v * jax.nn.sigmoid(v)


def _mm(a, b):
    return jnp.dot(a, b, preferred_element_type=F32)


def _mm_nt(a, b):
    return lax.dot_general(a, b, (((1,), (1,)), ((), ())), preferred_element_type=F32)


def _mm_tn(a, b):
    return lax.dot_general(a, b, (((0,), (0,)), ((), ())), preferred_element_type=F32)


def _split2(a):
    hi = a.astype(BF16)
    lo = (a - hi.astype(F32)).astype(BF16)
    return hi, lo


def _split3(a):
    hi = a.astype(BF16)
    r = a - hi.astype(F32)
    mid = r.astype(BF16)
    lo = (r - mid.astype(F32)).astype(BF16)
    return hi, mid, lo


def _mm_f32(a, b):
    ah, al = _split2(a)
    bh, bl = _split2(b)
    return _mm(ah, bh) + (_mm(ah, bl) + _mm(al, bh))


def _cumsum_rows(tri, a):
    hi, mid, lo = _split3(a)
    return _mm(jnp.concatenate([tri, tri], axis=1), jnp.concatenate([hi, mid], axis=0)) + _mm(tri, lo)


def _expand(a, e):
    hi, lo = _split2(a)
    return _mm(jnp.concatenate([hi, lo], axis=1), jnp.concatenate([e, e], axis=0))


def _block_diag(a):
    d = a.shape[1] // 2
    zero = jnp.zeros((a.shape[0], d), a.dtype)
    return jnp.concatenate([jnp.concatenate([a[:, :d], zero], axis=1),
                            jnp.concatenate([zero, a[:, d:]], axis=1)], axis=0)


def _iota(shape, dim):
    return lax.broadcasted_iota(jnp.int32, shape, dim)


def _tri(n):
    return jnp.where(_iota((n, n), 1) <= _iota((n, n), 0), 1.0, 0.0).astype(BF16)


def _head_expander(width):
    shape = (LANES, BRANCH_WIDTH)
    shift = int(math.log2(width))
    return jnp.where(jnp.right_shift(_iota(shape, 1), shift) == _iota(shape, 0), 1.0, 0.0).astype(BF16)


def _layer_norm_rows(v, g, b):
    vc = v - jnp.mean(v, axis=-1, keepdims=True)
    return vc * lax.rsqrt(jnp.mean(vc * vc, axis=-1, keepdims=True) + NORM_EPS) * g + b


def _mod_kernel(c_ref, w_ref, b_ref, o_ref):
    o_ref[0] = _mm_f32(_silu(c_ref[...]), w_ref[0]) + b_ref[0]


def _modulation(c, w_ada, b_ada):
    nb = c.shape[0]
    tn = 1536
    return pl.pallas_call(
        _mod_kernel,
        grid=(DEPTH, 6 * D_MODEL // tn),
        in_specs=[pl.BlockSpec((nb, D_MODEL), lambda l, j: (0, 0)),
                  pl.BlockSpec((1, D_MODEL, tn), lambda l, j: (l, 0, j)),
                  pl.BlockSpec((1, 1, tn), lambda l, j: (l, 0, j))],
        out_specs=pl.BlockSpec((1, nb, tn), lambda l, j: (l, 0, j)),
        out_shape=jax.ShapeDtypeStruct((DEPTH, nb, 6 * D_MODEL), F32),
        compiler_params=_cparams("parallel", "parallel"),
        name="adaln_mod",
    )(c, w_ada, b_ada.reshape(DEPTH, 1, 6 * D_MODEL))


def _const_spec(shape):
    zeros = (0,) * len(shape)
    return pl.BlockSpec(shape, lambda *_: zeros, pipeline_mode=pl.Buffered(1))


def _destride_matrix(n, r):
    per = n // r
    i = _iota((n, n), 0)
    src = jnp.left_shift(jnp.bitwise_and(i, per - 1), int(math.log2(r))) + jnp.right_shift(i, int(math.log2(per)))
    return jnp.where(_iota((n, n), 1) == src, 1.0, 0.0).astype(BF16)


def _proj_kernel(x_ref, mod_ref, wd_ref, od0_ref, od1_ref, od2_ref):
    m = mod_ref[...]
    h = (x_ref[...] * (1.0 + m[1:2]) + m[0:1]).astype(BF16)
    d = _mm(h, wd_ref[...]).astype(BF16)
    w = BRANCH_WIDTH
    tm = d.shape[0]
    od0_ref[...] = jnp.concatenate([d[:, 0:w], d[:, 2 * w:4 * w]], axis=1)
    od1_ref[...] = _mm(_destride_matrix(tm, DIL_PATTERNS[1][1]), d[:, w:4 * w]).astype(BF16)
    od2_ref[...] = _mm(_destride_matrix(tm, DIL_PATTERNS[2][1]), d[:, 2 * w:5 * w]).astype(BF16)


PROJ_ROWS = 256
DIL_OPERAND_COLS = ((0, 1, 2), (0, 1, 2), (2, 0, 1))


def _project_dilated(x2, mod, wd, seq):
    t = x2.shape[0]
    tm = PROJ_ROWS
    per_b = seq // tm
    qkv = 3 * BRANCH_WIDTH
    return pl.pallas_call(
        _proj_kernel,
        grid=(t // tm,),
        in_specs=[pl.BlockSpec((tm, D_MODEL), lambda i: (i, 0)),
                  pl.BlockSpec((None, 6, D_MODEL), lambda i: (i // per_b, 0, 0)),
                  _const_spec(wd.shape)],
        out_specs=[pl.BlockSpec((tm, qkv), lambda i: (i, 0))] * DIL_GROUPS,
        out_shape=[jax.ShapeDtypeStruct((t, qkv), BF16)] * DIL_GROUPS,
        compiler_params=_cparams("parallel"),
        name="in_proj_dilated",
    )(x2, mod, wd)


SSD_HALO = 8


def _ssd_init(xbuf, st):
    xbuf[:, 0:SSD_HALO, :] = jnp.zeros((xbuf.shape[0], SSD_HALO, SSD_CONV_DIM), F32)
    st[...] = jnp.zeros_like(st)


def _ssd_chunk(p_ref, o_ref, half, cw_ref, cb_ref, dtb_ref, alog_ref, dsk_ref, nw_ref, xbuf, st):
    c = CHUNK
    halo = SSD_HALO
    xbc0 = BRANCH_WIDTH
    dt0 = BRANCH_WIDTH + SSD_CONV_DIM
    for b in range(p_ref.shape[0]):
        xbuf[b, halo:halo + c, :] = p_ref[b, :, xbc0:dt0]
        acc = jnp.broadcast_to(cb_ref[...], (c, SSD_CONV_DIM))
        for k in range(SSD_CONV):
            off = halo - (SSD_CONV - 1) + k
            acc = acc + cw_ref[k:k + 1, :] * xbuf[b, off:off + c, :]
        xbuf[b, 0:halo, :] = xbuf[b, c:c + halo, :]
        y = _silu(acc)
        xs = y[:, :BRANCH_WIDTH]
        bm = y[:, BRANCH_WIDTH:BRANCH_WIDTH + LANES]
        cm = y[:, BRANCH_WIDTH + LANES:]

        dtr = p_ref[b, :, dt0:dt0 + LANES] + dtb_ref[...]
        dt = jnp.maximum(dtr, 0.0) + jnp.log1p(jnp.exp(-jnp.abs(dtr)))
        da = dt * (-jnp.exp(alog_ref[...]))
        cs = _cumsum_rows(_tri(c), da)
        cs_t = cs.T
        tot = cs[c - 1:c, :]

        stack = jnp.concatenate([dt, jnp.exp(tot - cs), jnp.exp(cs), jnp.broadcast_to(jnp.exp(tot), (8, LANES))], axis=0)
        ex = _expand(stack, _head_expander(SSD_HEAD_DIM))
        dt_e, ds_e, ecs_e, dec_e = ex[0:c], ex[c:2 * c], ex[2 * c:3 * c], ex[3 * c:3 * c + 1]

        xdt = xs * dt_e
        xds = xdt * ds_e
        causal = _iota((c, c), 1) <= _iota((c, c), 0)
        lane = _iota((c, LANES), 1)
        bm16 = bm.astype(BF16)
        cbs = []
        for g in range(SSD_GROUPS):
            cm_g = jnp.where(jnp.right_shift(lane, 6) == g, cm, 0.0).astype(BF16)
            cbs.append(_mm_nt(cm_g, bm16))
        parts = []
        for p in range(SSD_HEADS // 2):
            g = (2 * p) // (SSD_HEADS // SSD_GROUPS)
            ms = []
            for e in range(2):
                h = 2 * p + e
                diff = cs[:, h:h + 1] - cs_t[h:h + 1, :]
                ms.append((cbs[g] * jnp.exp(jnp.where(causal, diff, -jnp.inf))).astype(BF16))
            xp = xdt[:, p * LANES:(p + 1) * LANES]
            xbd = jnp.concatenate([jnp.where(lane < SSD_HEAD_DIM, xp, 0.0), jnp.where(lane >= SSD_HEAD_DIM, xp, 0.0)],
                                  axis=0).astype(BF16)
            parts.append(_mm(jnp.concatenate(ms, axis=1), xbd))
        y_diag = jnp.concatenate(parts, axis=1)

        s_prev = st[b]
        y_off = _mm(cm.astype(BF16), s_prev.astype(BF16)) * ecs_e
        upd = _mm(bm.T.astype(BF16), xds.astype(BF16))
        shape = (LANES, BRANCH_WIDTH)
        same_group = jnp.right_shift(_iota(shape, 0), 6) == jnp.right_shift(_iota(shape, 1), 8)
        st[b] = s_prev * dec_e + jnp.where(same_group, upd, 0.0)

        yv = (y_diag + y_off + xs * dsk_ref[...]) * _silu(p_ref[b, :, 0:BRANCH_WIDTH])
        gw = BRANCH_WIDTH // SSD_GROUPS
        outs = []
        for g in range(SSD_GROUPS):
            yg = yv[:, g * gw:(g + 1) * gw]
            outs.append(yg * lax.rsqrt(jnp.mean(yg * yg, axis=-1, keepdims=True) + NORM_EPS))
        o_ref[b, half * c:(half + 1) * c, :] = (jnp.concatenate(outs, axis=1) * nw_ref[...]).astype(o_ref.dtype)


def _ssd(x, mod, w, conv_w, conv_b, dt_bias, a_log, d_skip, norm_w):
    pad = LANES - SSD_HEADS
    row = lambda v: v.reshape(1, -1).astype(F32)
    aux = (conv_w.astype(F32), row(conv_b), row(jnp.pad(dt_bias, (0, pad))), row(jnp.pad(a_log, (0, pad))),
           row(jnp.repeat(d_skip, SSD_HEAD_DIM)), row(norm_w))
    return _fused_mixer("ssd_mixer", _ssd_init, _ssd_chunk, x, mod, w, aux,
                        [(CHUNK + SSD_HALO, SSD_CONV_DIM), (LANES, BRANCH_WIDTH)])


def _alibi_slopes(n):
    def pow2(k):
        start = 2.0 ** (-8.0 / k)
        return [start ** (i + 1) for i in range(k)]
    if math.log2(n).is_integer():
        s = pow2(n)
    else:
        c = 2 ** math.floor(math.log2(n))
        s = pow2(c) + pow2(2 * c)[0::2][: n - c]
    return [float(np.float32(v)) for v in s]


def _dil_kernel(dilation, n_back, slopes, q_ref, kc_ref, kp_ref, vc_ref, vp_ref, o_ref, lse_ref):
    blk = DIL_BLOCK
    rho = pl.program_id(2)
    log2e = 1.0 / math.log(2.0)

    def rows_of(ref, start, sl):
        if len(ref.shape) == 2:
            return ref[start:start + blk, sl]
        per = ref.shape[1]
        return jnp.concatenate([ref[t, :, sl] for t in range(start // per, (start + blk) // per)], axis=0)

    qi = _iota((blk, 2 * blk), 0)
    kj = _iota((blk, 2 * blk), 1)
    dist = qi - kj + blk
    in_window = (dist >= 0) & (dist <= n_back)
    neg_dist2 = (dist * dilation).astype(F32) * (-log2e)
    lane = _iota((blk, LANES), 1)
    for sb in range(DIL_QBLOCKS):
        if sb == 0:
            valid = in_window & (kj >= jnp.where(pl.program_id(1) > 0, 0, blk))
        else:
            valid = in_window
        bias = jnp.where(valid, neg_dist2, -jnp.inf)
        rows = pl.ds(rho + sb * blk * dilation, blk, stride=dilation) if dilation > 1 else pl.ds(sb * blk, blk)
        lse_tile = jnp.zeros((blk, LANES), F32)
        for p in range(DIL_HEADS // 2):
            sl = slice(p * LANES, (p + 1) * LANES)
            qp = rows_of(q_ref, sb * blk, sl).astype(F32) * (DIL_HEAD_DIM ** -0.5 * log2e)
            if sb == 0:
                kk = jnp.concatenate([rows_of(kp_ref, 0, sl), rows_of(kc_ref, 0, sl)], axis=0)
                vv = jnp.concatenate([rows_of(vp_ref, 0, sl), rows_of(vc_ref, 0, sl)], axis=0)
            else:
                kk = jnp.concatenate([rows_of(kc_ref, (sb - 1) * blk, sl), rows_of(kc_ref, sb * blk, sl)], axis=0)
                vv = jnp.concatenate([rows_of(vc_ref, (sb - 1) * blk, sl), rows_of(vc_ref, sb * blk, sl)], axis=0)
            pair = []
            for e in range(2):
                h = 2 * p + e
                own = (lane < DIL_HEAD_DIM) if e == 0 else (lane >= DIL_HEAD_DIM)
                qm = jnp.where(own, qp, 0.0).astype(BF16)
                s = _mm_nt(qm, kk) + slopes[h] * bias
                m = jnp.max(s, axis=-1, keepdims=True)
                pexp = jnp.exp2(s - m)
                l = jnp.sum(pexp, axis=-1, keepdims=True)
                pair.append(_mm(pexp.astype(BF16), vv) / l)
                lse_tile = jnp.where(lane == h, (m + jnp.log2(l)) * math.log(2.0), lse_tile)
            o_ref[p, rows, :] = jnp.where(lane < DIL_HEAD_DIM, pair[0], pair[1])
        lse_ref[rows, :] = lse_tile


DIL_QBLOCKS = 2


def _dilated_group(pd, g):
    nb, seq, _ = pd.shape
    window, r = DIL_PATTERNS[g]
    n = seq // r
    qrows = DIL_QBLOCKS * DIL_BLOCK
    w = BRANCH_WIDTH
    qc, kc, vc = DIL_OPERAND_COLS[g]
    if r == 1:
        view = pd
        cur = lambda col: pl.BlockSpec((None, qrows, w), lambda b, i, rho: (b, i, col))
        prev = lambda col: pl.BlockSpec((None, DIL_BLOCK, w),
                                        lambda b, i, rho: (b, jnp.maximum(DIL_QBLOCKS * i - 1, 0), col))
    else:
        per = PROJ_ROWS // r
        view = pd.reshape(nb, seq // PROJ_ROWS, r, per, 3 * w)
        cur = lambda col: pl.BlockSpec((None, qrows // per, None, per, w), lambda b, i, rho: (b, i, rho, 0, col))
        prev = lambda col: pl.BlockSpec((None, DIL_BLOCK // per, None, per, w),
                                        lambda b, i, rho: (b, jnp.maximum(DIL_QBLOCKS * i - 1, 0), rho, 0, col))
    slopes = _alibi_slopes(DIL_GROUPS * DIL_HEADS)[g * DIL_HEADS:(g + 1) * DIL_HEADS]
    pairs = DIL_HEADS // 2
    return pl.pallas_call(
        functools.partial(_dil_kernel, r, window // r, slopes),
        grid=(nb, n // qrows, r),
        in_specs=[cur(qc), cur(kc), prev(kc), cur(vc), prev(vc)],
        out_specs=[pl.BlockSpec((pairs, None, qrows * r, LANES), lambda b, i, rho: (0, b, i, 0)),
                   pl.BlockSpec((None, qrows * r, LANES), lambda b, i, rho: (b, i, 0))],
        out_shape=[jax.ShapeDtypeStruct((pairs, nb, seq, LANES), F32),
                   jax.ShapeDtypeStruct((nb, seq, LANES), F32)],
        compiler_params=_cparams("parallel", "parallel", "arbitrary"),
        name=f"dilated_attn_g{g}",
    )(view, view, view, view, view)


def _combine_dilated(lses, outs):
    m = functools.reduce(jnp.maximum, lses)
    es = [jnp.exp(l - m) for l in lses]
    den = functools.reduce(lambda a, b: a + b, es)
    e = _head_expander(DIL_HEAD_DIM)
    acc = None
    for ev, o in zip(es, outs):
        term = _expand(ev / den, e) * jnp.concatenate(o, axis=1)
        acc = term if acc is None else acc + term
    return acc


def _dilated(pds):
    nb, seq, _ = pds[0].shape
    res = [_dilated_group(pd, g) for g, pd in enumerate(pds)]
    t = nb * seq
    return [r[1].reshape(t, LANES) for r in res], [r[0].reshape(DIL_HEADS // 2, t, LANES) for r in res]


def _fused_mixer_kernel(init_fn, chunk_fn, n_aux, x0_ref, xa_ref, xb_ref, mod_ref, w_ref, *rest):
    aux, o_ref, p0, p1, scratch = rest[:n_aux], rest[n_aux], rest[n_aux + 1], rest[n_aux + 2], rest[n_aux + 3:]
    rows = x0_ref.shape[0]
    m = mod_ref[...]

    def project(x_ref, p_ref):
        h = jnp.concatenate([(x_ref[b] * (1.0 + m[b, 1:2]) + m[b, 0:1]).astype(BF16) for b in range(rows)], axis=0)
        res = _mm(h, w_ref[...])
        for b in range(rows):
            p_ref[b] = res[b * CHUNK:(b + 1) * CHUNK]

    @pl.when(pl.program_id(1) == 0)
    def _():
        init_fn(*scratch)
        project(x0_ref, p0)

    chunk_fn(p0, o_ref, 0, *aux, *scratch)
    project(xa_ref, p1)
    chunk_fn(p1, o_ref, 1, *aux, *scratch)
    project(xb_ref, p0)


def _fused_mixer(name, init_fn, chunk_fn, x, mod, w, aux, scratch_shapes):
    nb, seq, _ = x.shape
    rows = _rows_per_step(nb)
    nc = seq // CHUNK
    n = w.shape[1]
    xspec = lambda chunk_of: pl.BlockSpec((rows, CHUNK, D_MODEL), lambda b, s: (b, chunk_of(s), 0))
    return pl.pallas_call(
        functools.partial(_fused_mixer_kernel, init_fn, chunk_fn, len(aux)),
        grid=(nb // rows, nc // 2),
        in_specs=[xspec(lambda s: 0), xspec(lambda s: 2 * s + 1), xspec(lambda s: jnp.minimum(2 * s + 2, nc - 1)),
                  pl.BlockSpec((rows, 6, D_MODEL), lambda b, s: (b, 0, 0)), _const_spec(w.shape)]
        + [_const_spec(a.shape) for a in aux],
        out_specs=pl.BlockSpec((rows, 2 * CHUNK, BRANCH_WIDTH), lambda b, s: (b, s, 0)),
        out_shape=jax.ShapeDtypeStruct((nb, seq, BRANCH_WIDTH), BF16),
        scratch_shapes=[pltpu.VMEM((rows, CHUNK, n), F32), pltpu.VMEM((rows, CHUNK, n), F32)]
        + [pltpu.VMEM((rows,) + s, F32) for s in scratch_shapes],
        compiler_params=_cparams("parallel", "arbitrary"),
        name=name,
    )(x, x, x, mod, w, *aux)


def _ret_init(st):
    st[...] = jnp.zeros_like(st)


def _ret_chunk(p_ref, o_ref, half, st):
    c = CHUNK
    w = BRANCH_WIDTH
    dk = RET_DK
    scale = dk ** -0.5
    row = _iota((c, c), 0)
    col = _iota((c, c), 1)
    rel = (row - col).astype(F32)
    pos = row.astype(F32)
    lgs = [math.log(1.0 - 2.0 ** (-5.0 - h)) for h in range(RET_HEADS)]

    def pair_of(f):
        return [jnp.concatenate([f(lgs[2 * pr]), f(lgs[2 * pr + 1])], axis=1) for pr in range(RET_HEADS // 2)]

    decay = pair_of(lambda lg: jnp.where(row >= col, jnp.exp(lg * jnp.maximum(rel, 0.0)), 0.0) * scale)
    q_scale = pair_of(lambda lg: jnp.exp(lg * (pos + 1.0)))
    k_scale = pair_of(lambda lg: jnp.exp(lg * (c - 1.0 - pos)) * scale)
    on_diag = jnp.right_shift(_iota((2 * dk, 2 * dk), 0), 7) == jnp.right_shift(_iota((2 * dk, 2 * dk), 1), 7)
    upper = _iota((2 * dk, 2 * dk), 0) < dk
    for b in range(p_ref.shape[0]):
        outs = []
        for pr in range(RET_HEADS // 2):
            sl = lambda i: slice(i * w + pr * 2 * dk, i * w + (pr + 1) * 2 * dk)
            qf, kf = p_ref[b, :, sl(0)], p_ref[b, :, sl(1)]
            v16 = p_ref[b, :, sl(2)].astype(BF16)
            scores = _mm_nt(qf.astype(BF16), _block_diag(kf.astype(BF16))) * decay[pr]
            inner = _mm(scores.astype(BF16), _block_diag(v16))
            s_prev = st[b, pr]
            o = inner + _mm((qf * q_scale[pr]).astype(BF16), s_prev.astype(BF16))
            upd = _mm_tn((kf * k_scale[pr]).astype(BF16), v16)
            carry = jnp.where(upper, math.exp(lgs[2 * pr] * c), math.exp(lgs[2 * pr + 1] * c))
            st[b, pr] = s_prev * carry + jnp.where(on_diag, upd, 0.0)
            for e in range(2):
                oh = o[:, e * dk:(e + 1) * dk]
                oc = oh - jnp.mean(oh, axis=-1, keepdims=True)
                outs.append(oc * lax.rsqrt(jnp.mean(oc * oc, axis=-1, keepdims=True) + NORM_EPS))
        gate = _silu(p_ref[b, :, 3 * w:4 * w])
        o_ref[b, half * c:(half + 1) * c, :] = (jnp.concatenate(outs, axis=1) * gate).astype(o_ref.dtype)


def _retention(x, mod, w):
    return _fused_mixer("retention", _ret_init, _ret_chunk, x, mod, w, (),
                        [(RET_HEADS // 2, 2 * RET_DK, 2 * RET_DK)])


def _hgrn_init(st):
    st[...] = jnp.zeros_like(st)


def _hgrn_chunk(layer, p_ref, o_ref, half, lb_ref, nw_ref, st):
    c = CHUNK
    width = BRANCH_WIDTH
    neg_inf = -jnp.inf

    rows = [lb_ref[l:l + 1, :] for l in range(DEPTH)]
    mx = functools.reduce(jnp.maximum, rows)
    es = [jnp.exp(r - mx) for r in rows]
    den = functools.reduce(lambda a, b: a + b, es)
    sm = [e / den for e in es]
    lb = functools.reduce(lambda a, b: a + b, sm[:layer + 1]) - sm[0]

    ti = _iota((c, c), 0)
    tj = _iota((c, c), 1)
    le_t = jnp.where(tj <= ti, 1.0, 0.0)
    sizes = [c >> k for k in range(int(math.log2(c // HGRN_SUB)))]
    mats = [le_t]
    for size in sizes:
        sh = int(math.log2(size))
        last_lower = jnp.left_shift(jnp.right_shift(ti, sh), sh) + (size // 2 - 1)
        mats.append(le_t - jnp.where(tj <= last_lower, 1.0, 0.0))
    seg = jnp.concatenate(mats, axis=0).astype(BF16)
    rowi = _iota((c, width), 0)
    sub = jnp.bitwise_and(rowi, HGRN_SUB - 1)
    heads = [slice(h * HGRN_DK, (h + 1) * HGRN_DK) for h in range(HGRN_HEADS)]

    for b in range(p_ref.shape[0]):
        forget = lb + (1.0 - lb) * jax.nn.sigmoid(p_ref[b, :, width:2 * width])
        lf = jnp.log(forget)
        kk = 1.0 - forget
        q = _silu(p_ref[b, :, 0:width])
        vf = p_ref[b, :, 2 * width:3 * width]
        v16 = vf.astype(BF16)
        sums = _cumsum_rows(seg, lf)
        lam = sums[0:c]

        o_acc = [jnp.zeros((c, HGRN_DK), F32) for _ in heads]
        for d in range(HGRN_SUB):
            if d == 0:
                prod = q * kk
                vd = vf
            else:
                ld = pltpu.roll(lam, d, 0)
                prod = q * pltpu.roll(kk, d, 0) * jnp.exp(jnp.where(sub >= d, lam - ld, neg_inf))
                vd = pltpu.roll(vf, d, 0)
            for h, sl in enumerate(heads):
                o_acc[h] = o_acc[h] + jnp.sum(prod[:, sl], axis=-1, keepdims=True) * vd[:, sl]

        attn = [jnp.zeros((c, c), F32) for _ in heads]
        for k, size in enumerate(sizes):
            mid = size // 2
            rel = sums[(k + 1) * c:(k + 2) * c]
            upper = jnp.bitwise_and(rowi, size - 1) >= mid
            z = (jnp.where(upper, q, kk) * jnp.exp(-jnp.abs(rel))).astype(BF16)
            sh = int(math.log2(size))
            pairs = ((jnp.right_shift(ti, sh) == jnp.right_shift(tj, sh))
                     & (jnp.bitwise_and(ti, size - 1) >= mid) & (jnp.bitwise_and(tj, size - 1) < mid))
            for h, sl in enumerate(heads):
                attn[h] = attn[h] + jnp.where(pairs, _mm_nt(z[:, sl], z[:, sl]), 0.0)

        lam_last = lam[c - 1:c, :]
        q_in = (q * jnp.exp(lam)).astype(BF16)
        k_out = (kk * jnp.exp(lam_last - lam)).astype(BF16)
        e_last = jnp.exp(lam_last)
        outs = []
        for h, sl in enumerate(heads):
            s_prev = st[b, h]
            o = o_acc[h] + _mm(attn[h].astype(BF16), v16[:, sl]) + _mm_nt(q_in[:, sl], s_prev.astype(BF16))
            st[b, h] = s_prev * e_last[:, sl] + _mm_tn(v16[:, sl], k_out[:, sl])
            outs.append(o * lax.rsqrt(jnp.mean(o * o, axis=-1, keepdims=True) + NORM_EPS) * nw_ref[...])
        gate = _silu(p_ref[b, :, 3 * width:4 * width])
        o_ref[b, half * c:(half + 1) * c, :] = (jnp.concatenate(outs, axis=1) * gate).astype(o_ref.dtype)


def _hgrn(x, mod, w, layer, hgrn_lb, norm_w):
    aux = (hgrn_lb.astype(F32), norm_w.reshape(1, HGRN_DK).astype(F32))
    return _fused_mixer("hgrn2", _hgrn_init, functools.partial(_hgrn_chunk, layer), x, mod, w, aux,
                        [(HGRN_HEADS, HGRN_DK, HGRN_DK)])


def _merge_kernel(x_ref, mod_ref, ssd_ref, l0_ref, l1_ref, l2_ref, d0_ref, d1_ref, d2_ref, ret_ref, hgrn_ref,
                  wm_ref, wb_ref, wo_ref, g_ref, b_ref, out_ref):
    m = mod_ref[...]
    x = x_ref[...]
    h = (x * (1.0 + m[1:2]) + m[0:1]).astype(BF16)
    pairs = range(DIL_HEADS // 2)
    dil = _combine_dilated([r[...] for r in (l0_ref, l1_ref, l2_ref)],
                           [[r[p] for p in pairs] for r in (d0_ref, d1_ref, d2_ref)]).astype(BF16)
    y = None
    for i, o in enumerate((ssd_ref[...], dil, ret_ref[...], hgrn_ref[...])):
        gate = jax.nn.sigmoid(_mm(h, wm_ref[:, i * D_MODEL:(i + 1) * D_MODEL]))
        term = gate * _mm(o, wb_ref[i])
        y = term if y is None else y + term
    mix = _mm(y.astype(BF16), wo_ref[...])
    out_ref[...] = _layer_norm_rows(DEEPNORM_ALPHA * x + m[2:3] * mix, g_ref[...], b_ref[...])


def _merge(x2, mod, o_ssd, dil, o_ret, o_hgrn, w_merge, w_branch, w_o, ln_g, ln_b, seq):
    t = x2.shape[0]
    tm = 256
    per_b = seq // tm
    lses, outs = dil
    rows = pl.BlockSpec((tm, D_MODEL), lambda i: (i, 0))
    brow = pl.BlockSpec((tm, BRANCH_WIDTH), lambda i: (i, 0))
    lrow = pl.BlockSpec((tm, LANES), lambda i: (i, 0))
    drow = pl.BlockSpec((DIL_HEADS // 2, tm, LANES), lambda i: (0, i, 0))
    return pl.pallas_call(
        _merge_kernel,
        grid=(t // tm,),
        in_specs=[rows, pl.BlockSpec((None, 6, D_MODEL), lambda i: (i // per_b, 0, 0)), brow]
        + [lrow] * DIL_GROUPS + [drow] * DIL_GROUPS + [brow, brow]
        + [_const_spec(w_merge.shape), _const_spec(w_branch.shape), _const_spec(w_o.shape),
           _const_spec((1, D_MODEL)), _const_spec((1, D_MODEL))],
        out_specs=rows,
        out_shape=jax.ShapeDtypeStruct((t, D_MODEL), F32),
        compiler_params=_cparams("parallel"),
        name="merge_out_ln",
    )(x2, mod, o_ssd, *lses, *outs, o_ret, o_hgrn, w_merge, w_branch, w_o, ln_g.reshape(1, -1), ln_b.reshape(1, -1))


def _ffn_kernel(x_ref, mod_ref, wu_ref, cw_ref, cb_ref, wd_ref, g_ref, b_ref, out_ref, ubuf0, ubuf1, tail, act):
    tm = x_ref.shape[0]
    halo = 8
    assert N_FF_CHUNKS % 2 == 1

    @pl.when(pl.program_id(1) == 0)
    def _():
        tail[...] = jnp.zeros_like(tail)

    m = mod_ref[...]
    x = x_ref[...]
    h = (x * (1.0 + m[4:5]) + m[3:4]).astype(BF16)

    def up(j, ubuf):
        ubuf[0:halo, :] = tail[j]
        ubuf[halo:halo + tm, :] = _mm(h, wu_ref[j])
        tail[j] = ubuf[tm:tm + halo, :]

    def gate(j, ubuf):
        cw = cw_ref[j]
        u = jnp.broadcast_to(cb_ref[j], (tm, 2 * FF_CHUNK))
        for k in range(FFN_CONV):
            off = halo - (FFN_CONV - 1) + k
            u = u + cw[k:k + 1, :] * ubuf[off:off + tm, :]
        act[j] = (_silu(u[:, :FF_CHUNK]) * u[:, FF_CHUNK:]).astype(BF16)

    bufs = (ubuf0, ubuf1)
    up(0, ubuf0)
    for j in range(N_FF_CHUNKS):
        if j + 1 < N_FF_CHUNKS:
            up(j + 1, bufs[(j + 1) % 2])
        gate(j, bufs[j % 2])
    ffn = _mm(act[0], wd_ref[0])
    for j in range(1, N_FF_CHUNKS):
        ffn = ffn + _mm(act[j], wd_ref[j])
    out_ref[...] = _layer_norm_rows(DEEPNORM_ALPHA * x + m[5:6] * ffn, g_ref[...], b_ref[...])


def _ffn(x2, mod, w_up, conv_w, conv_b, w_down, ln_g, ln_b, nb, seq):
    tm = 512
    per_b = seq // tm
    rows = pl.BlockSpec((tm, D_MODEL), lambda b, i: (b * per_b + i, 0))
    return pl.pallas_call(
        _ffn_kernel,
        grid=(nb, per_b),
        in_specs=[rows, pl.BlockSpec((None, 6, D_MODEL), lambda b, i: (b, 0, 0)),
                  _const_spec(w_up.shape), _const_spec(conv_w.shape), _const_spec(conv_b.shape),
                  _const_spec(w_down.shape), _const_spec((1, D_MODEL)), _const_spec((1, D_MODEL))],
        out_specs=rows,
        out_shape=jax.ShapeDtypeStruct((nb * seq, D_MODEL), F32),
        scratch_shapes=[pltpu.VMEM((tm + 8, 2 * FF_CHUNK), F32),
                        pltpu.VMEM((tm + 8, 2 * FF_CHUNK), F32),
                        pltpu.VMEM((N_FF_CHUNKS, 8, 2 * FF_CHUNK), F32),
                        pltpu.VMEM((N_FF_CHUNKS, tm, FF_CHUNK), BF16)],
        compiler_params=_cparams("parallel", "arbitrary"),
        name="conv_ffn_ln",
    )(x2, mod, w_up, conv_w, conv_b, w_down, ln_g.reshape(1, -1), ln_b.reshape(1, -1))


def _pack_in_proj(w_in):
    offs = np.concatenate([[0], np.cumsum(SECTION_SIZES)])
    sec = {n: w_in[:, offs[i]:offs[i + 1]] for i, n in enumerate(SECTION_NAMES)}
    w = BRANCH_WIDTH
    sec.update({f"q{g}": sec["dq"][:, g * w:(g + 1) * w] for g in range(DIL_GROUPS)}, k=sec["dk"], v=sec["dv"])
    sec["dt_pad"] = jnp.zeros((D_MODEL, LANES - SSD_HEADS), w_in.dtype)
    cat = lambda names: jnp.concatenate([sec[n] for n in names], axis=1).astype(BF16)
    return {"dil": cat(sorted(PD_COLS, key=PD_COLS.get)), "ssd": cat(("z", "xbc", "dt", "dt_pad")),
            "ret": cat(("rq", "rk", "rv", "rg")), "hgrn": cat(("hq", "hf", "hi", "hg"))}


def _chunk_gate_value(a):
    lead = a.shape[:-1]
    a = a.reshape(lead + (2, N_FF_CHUNKS, FF_CHUNK))
    a = jnp.moveaxis(a, -2, 0)
    return a.reshape((N_FF_CHUNKS,) + lead + (2 * FF_CHUNK,))


def kernel(x, c, w_ada, b_ada, w_in, w_merge, ssd_conv_w, ssd_conv_b, ssd_dt_bias, ssd_a_log, ssd_d, ssd_norm_w, hgrn_lb, hgrn_norm_w, w_branch_out, w_o, ln1_g, ln1_b, w_up, ffn_conv_w, ffn_conv_b, w_down, ln2_g, ln2_b):
    nb, seq, _ = x.shape
    assert seq % (DIL_PATTERNS[-1][1] * DIL_QBLOCKS * DIL_BLOCK) == 0 and x.shape[-1] == D_MODEL
    mods = _modulation(c, w_ada, b_ada).reshape(DEPTH, nb, 6, D_MODEL)
    x2 = x.reshape(nb * seq, D_MODEL)
    for layer in range(DEPTH):
        mod = mods[layer]
        w_proj = _pack_in_proj(w_in[layer])
        x3 = x2.reshape(nb, seq, D_MODEL)
        o_dil = _dilated([a.reshape(nb, seq, -1) for a in _project_dilated(x2, mod, w_proj["dil"], seq)])
        o_ssd = _ssd(x3, mod, w_proj["ssd"], ssd_conv_w[layer], ssd_conv_b[layer], ssd_dt_bias[layer],
                     ssd_a_log[layer], ssd_d[layer], ssd_norm_w[layer])
        o_ret = _retention(x3, mod, w_proj["ret"])
        o_hgrn = _hgrn(x3, mod, w_proj["hgrn"], layer, hgrn_lb, hgrn_norm_w[layer])
        flat = lambda a: a.reshape(nb * seq, BRANCH_WIDTH)
        x2 = _merge(x2, mod, flat(o_ssd), o_dil, flat(o_ret), flat(o_hgrn), w_merge[layer].astype(BF16),
                    w_branch_out[layer].astype(BF16), w_o[layer].astype(BF16), ln1_g[layer], ln1_b[layer], seq)
        x2 = _ffn(x2, mod, _chunk_gate_value(w_up[layer]).astype(BF16), _chunk_gate_value(ffn_conv_w[layer]),
                  _chunk_gate_value(ffn_conv_b[layer].reshape(1, -1)), w_down[layer].reshape(N_FF_CHUNKS, FF_CHUNK, D_MODEL).astype(BF16),
                  ln2_g[layer], ln2_b[layer], nb, seq)
    return x2.reshape(nb, seq, D_MODEL)
```

```python
import functools
import math

import numpy as np
import jax
import jax.numpy as jnp
from jax import lax
from jax.experimental import pallas as pl
from jax.experimental.pallas import tpu as pltpu

F32 = jnp.float32
BF16 = jnp.bfloat16

D_MODEL = 1024
DEPTH = 2
DEEPNORM_ALPHA = (2 * DEPTH) ** 0.25
NORM_EPS = 1e-5

N_BRANCH = 4
BRANCH_WIDTH = D_MODEL // 2

SSD_HEAD_DIM = 64
SSD_HEADS = BRANCH_WIDTH // SSD_HEAD_DIM
SSD_GROUPS = 2
SSD_STATE = 64
SSD_CONV = 4
SSD_CONV_DIM = BRANCH_WIDTH + 2 * SSD_GROUPS * SSD_STATE

DIL_HEAD_DIM = 64
DIL_HEADS = BRANCH_WIDTH // DIL_HEAD_DIM
DIL_PATTERNS = ((128, 1), (512, 4), (2048, 16))
DIL_GROUPS = len(DIL_PATTERNS)
DIL_BLOCK = 128

RET_HEADS = 4
RET_DK = BRANCH_WIDTH // RET_HEADS

HGRN_HEADS = 4
HGRN_DK = BRANCH_WIDTH // HGRN_HEADS
HGRN_SUB = 4

D_FF = 256 * ((8 * D_MODEL // 3 + 255) // 256)
FFN_CONV = 3
FF_CHUNK = 256
N_FF_CHUNKS = D_FF // FF_CHUNK

SECTION_SIZES = (
    BRANCH_WIDTH, SSD_CONV_DIM, SSD_HEADS,
    DIL_GROUPS * BRANCH_WIDTH, BRANCH_WIDTH, BRANCH_WIDTH,
    BRANCH_WIDTH, BRANCH_WIDTH, BRANCH_WIDTH, BRANCH_WIDTH,
    BRANCH_WIDTH, BRANCH_WIDTH, BRANCH_WIDTH, BRANCH_WIDTH,
)
SECTION_NAMES = ("z", "xbc", "dt", "dq", "dk", "dv", "rq", "rk", "rv", "rg", "hq", "hf", "hi", "hg")

LANES = 128
CHUNK = 128
PD_COLS = {n: i for i, n in enumerate(("q0", "q1", "k", "v", "q2"))}
VMEM_LIMIT = 56 * 1024 * 1024


MIXER_ROWS = 2


def _rows_per_step(nb):
    return MIXER_ROWS if nb % MIXER_ROWS == 0 else 1


def _cparams(*sem):
    return pltpu.CompilerParams(dimension_semantics=sem, vmem_limit_bytes=VMEM_LIMIT)


def _silu(v):
    return v * jax.nn.sigmoid(v)


def _mm(a, b):
    return jnp.dot(a, b, preferred_element_type=F32)


def _mm_nt(a, b):
    return lax.dot_general(a, b, (((1,), (1,)), ((), ())), preferred_element_type=F32)


def _mm_tn(a, b):
    return lax.dot_general(a, b, (((0,), (0,)), ((), ())), preferred_element_type=F32)


def _split2(a):
    hi = a.astype(BF16)
    lo = (a - hi.astype(F32)).astype(BF16)
    return hi, lo


def _split3(a):
    hi = a.astype(BF16)
    r = a - hi.astype(F32)
    mid = r.astype(BF16)
    lo = (r - mid.astype(F32)).astype(BF16)
    return hi, mid, lo


def _mm_f32(a, b):
    ah, al = _split2(a)
    bh, bl = _split2(b)
    return _mm(ah, bh) + (_mm(ah, bl) + _mm(al, bh))


def _cumsum_rows(tri, a):
    hi, mid, lo = _split3(a)
    return _mm(jnp.concatenate([tri, tri], axis=1), jnp.concatenate([hi, mid], axis=0)) + _mm(tri, lo)


def _expand(a, e):
    hi, lo = _split2(a)
    return _mm(jnp.concatenate([hi, lo], axis=1), jnp.concatenate([e, e], axis=0))


def _block_diag(a):
    d = a.shape[1] // 2
    zero = jnp.zeros((a.shape[0], d), a.dtype)
    return jnp.concatenate([jnp.concatenate([a[:, :d], zero], axis=1),
                            jnp.concatenate([zero, a[:, d:]], axis=1)], axis=0)


def _iota(shape, dim):
    return lax.broadcasted_iota(jnp.int32, shape, dim)


def _tri(n):
    return jnp.where(_iota((n, n), 1) <= _iota((n, n), 0), 1.0, 0.0).astype(BF16)


def _head_expander(width):
    shape = (LANES, BRANCH_WIDTH)
    shift = int(math.log2(width))
    return jnp.where(jnp.right_shift(_iota(shape, 1), shift) == _iota(shape, 0), 1.0, 0.0).astype(BF16)


def _layer_norm_rows(v, g, b):
    vc = v - jnp.mean(v, axis=-1, keepdims=True)
    return vc * lax.rsqrt(jnp.mean(vc * vc, axis=-1, keepdims=True) + NORM_EPS) * g + b


def _mod_kernel(c_ref, w_ref, b_ref, o_ref):
    o_ref[0] = _mm_f32(_silu(c_ref[...]), w_ref[0]) + b_ref[0]


def _modulation(c, w_ada, b_ada):
    nb = c.shape[0]
    tn = 1536
    return pl.pallas_call(
        _mod_kernel,
        grid=(DEPTH, 6 * D_MODEL // tn),
        in_specs=[pl.BlockSpec((nb, D_MODEL), lambda l, j: (0, 0)),
                  pl.BlockSpec((1, D_MODEL, tn), lambda l, j: (l, 0, j)),
                  pl.BlockSpec((1, 1, tn), lambda l, j: (l, 0, j))],
        out_specs=pl.BlockSpec((1, nb, tn), lambda l, j: (l, 0, j)),
        out_shape=jax.ShapeDtypeStruct((DEPTH, nb, 6 * D_MODEL), F32),
        compiler_params=_cparams("parallel", "parallel"),
        name="adaln_mod",
    )(c, w_ada, b_ada.reshape(DEPTH, 1, 6 * D_MODEL))


def _const_spec(shape):
    zeros = (0,) * len(shape)
    return pl.BlockSpec(shape, lambda *_: zeros, pipeline_mode=pl.Buffered(1))


def _destride_matrix(n, r):
    per = n // r
    i = _iota((n, n), 0)
    src = jnp.left_shift(jnp.bitwise_and(i, per - 1), int(math.log2(r))) + jnp.right_shift(i, int(math.log2(per)))
    return jnp.where(_iota((n, n), 1) == src, 1.0, 0.0).astype(BF16)


def _proj_kernel(x_ref, mod_ref, wd_ref, od0_ref, od1_ref, od2_ref):
    m = mod_ref[...]
    h = (x_ref[...] * (1.0 + m[1:2]) + m[0:1]).astype(BF16)
    d = _mm(h, wd_ref[...]).astype(BF16)
    w = BRANCH_WIDTH
    tm = d.shape[0]
    od0_ref[...] = jnp.concatenate([d[:, 0:w], d[:, 2 * w:4 * w]], axis=1)
    od1_ref[...] = _mm(_destride_matrix(tm, DIL_PATTERNS[1][1]), d[:, w:4 * w]).astype(BF16)
    od2_ref[...] = _mm(_destride_matrix(tm, DIL_PATTERNS[2][1]), d[:, 2 * w:5 * w]).astype(BF16)


PROJ_ROWS = 256
DIL_OPERAND_COLS = ((0, 1, 2), (0, 1, 2), (2, 0, 1))


def _project_dilated(x2, mod, wd, seq):
    t = x2.shape[0]
    tm = PROJ_ROWS
    per_b = seq // tm
    qkv = 3 * BRANCH_WIDTH
    return pl.pallas_call(
        _proj_kernel,
        grid=(t // tm,),
        in_specs=[pl.BlockSpec((tm, D_MODEL), lambda i: (i, 0)),
                  pl.BlockSpec((None, 6, D_MODEL), lambda i: (i // per_b, 0, 0)),
                  _const_spec(wd.shape)],
        out_specs=[pl.BlockSpec((tm, qkv), lambda i: (i, 0))] * DIL_GROUPS,
        out_shape=[jax.ShapeDtypeStruct((t, qkv), BF16)] * DIL_GROUPS,
        compiler_params=_cparams("parallel"),
        name="in_proj_dilated",
    )(x2, mod, wd)


SSD_HALO = 8


def _ssd_init(xbuf, st):
    xbuf[:, 0:SSD_HALO, :] = jnp.zeros((xbuf.shape[0], SSD_HALO, SSD_CONV_DIM), F32)
    st[...] = jnp.zeros_like(st)


def _ssd_chunk(p_ref, o_ref, half, cw_ref, cb_ref, dtb_ref, alog_ref, dsk_ref, nw_ref, xbuf, st):
    c = CHUNK
    halo = SSD_HALO
    xbc0 = BRANCH_WIDTH
    dt0 = BRANCH_WIDTH + SSD_CONV_DIM
    for b in range(p_ref.shape[0]):
        xbuf[b, halo:halo + c, :] = p_ref[b, :, xbc0:dt0]
        acc = jnp.broadcast_to(cb_ref[...], (c, SSD_CONV_DIM))
        for k in range(SSD_CONV):
            off = halo - (SSD_CONV - 1) + k
            acc = acc + cw_ref[k:k + 1, :] * xbuf[b, off:off + c, :]
        xbuf[b, 0:halo, :] = xbuf[b, c:c + halo, :]
        y = _silu(acc)
        xs = y[:, :BRANCH_WIDTH]
        bm = y[:, BRANCH_WIDTH:BRANCH_WIDTH + LANES]
        cm = y[:, BRANCH_WIDTH + LANES:]

        dtr = p_ref[b, :, dt0:dt0 + LANES] + dtb_ref[...]
        dt = jnp.maximum(dtr, 0.0) + jnp.log1p(jnp.exp(-jnp.abs(dtr)))
        da = dt * (-jnp.exp(alog_ref[...]))
        cs = _cumsum_rows(_tri(c), da)
        cs_t = cs.T
        tot = cs[c - 1:c, :]

        stack = jnp.concatenate([dt, jnp.exp(tot - cs), jnp.exp(cs), jnp.broadcast_to(jnp.exp(tot), (8, LANES))], axis=0)
        ex = _expand(stack, _head_expander(SSD_HEAD_DIM))
        dt_e, ds_e, ecs_e, dec_e = ex[0:c], ex[c:2 * c], ex[2 * c:3 * c], ex[3 * c:3 * c + 1]

        xdt = xs * dt_e
        xds = xdt * ds_e
        causal = _iota((c, c), 1) <= _iota((c, c), 0)
        lane = _iota((c, LANES), 1)
        bm16 = bm.astype(BF16)
        cbs = []
        for g in range(SSD_GROUPS):
            cm_g = jnp.where(jnp.right_shift(lane, 6) == g, cm, 0.0).astype(BF16)
            cbs.append(_mm_nt(cm_g, bm16))
        parts = []
        for p in range(SSD_HEADS // 2):
            g = (2 * p) // (SSD_HEADS // SSD_GROUPS)
            ms = []
            for e in range(2):
                h = 2 * p + e
                diff = cs[:, h:h + 1] - cs_t[h:h + 1, :]
                ms.append((cbs[g] * jnp.exp(jnp.where(causal, diff, -jnp.inf))).astype(BF16))
            xp = xdt[:, p * LANES:(p + 1) * LANES]
            xbd = jnp.concatenate([jnp.where(lane < SSD_HEAD_DIM, xp, 0.0), jnp.where(lane >= SSD_HEAD_DIM, xp, 0.0)],
                                  axis=0).astype(BF16)
            parts.append(_mm(jnp.concatenate(ms, axis=1), xbd))
        y_diag = jnp.concatenate(parts, axis=1)

        s_prev = st[b]
        y_off = _mm(cm.astype(BF16), s_prev.astype(BF16)) * ecs_e
        upd = _mm(bm.T.astype(BF16), xds.astype(BF16))
        shape = (LANES, BRANCH_WIDTH)
        same_group = jnp.right_shift(_iota(shape, 0), 6) == jnp.right_shift(_iota(shape, 1), 8)
        st[b] = s_prev * dec_e + jnp.where(same_group, upd, 0.0)

        yv = (y_diag + y_off + xs * dsk_ref[...]) * _silu(p_ref[b, :, 0:BRANCH_WIDTH])
        gw = BRANCH_WIDTH // SSD_GROUPS
        outs = []
        for g in range(SSD_GROUPS):
            yg = yv[:, g * gw:(g + 1) * gw]
            outs.append(yg * lax.rsqrt(jnp.mean(yg * yg, axis=-1, keepdims=True) + NORM_EPS))
        o_ref[b, half * c:(half + 1) * c, :] = (jnp.concatenate(outs, axis=1) * nw_ref[...]).astype(o_ref.dtype)


def _ssd(x, mod, w, conv_w, conv_b, dt_bias, a_log, d_skip, norm_w):
    pad = LANES - SSD_HEADS
    row = lambda v: v.reshape(1, -1).astype(F32)
    aux = (conv_w.astype(F32), row(conv_b), row(jnp.pad(dt_bias, (0, pad))), row(jnp.pad(a_log, (0, pad))),
           row(jnp.repeat(d_skip, SSD_HEAD_DIM)), row(norm_w))
    return _fused_mixer("ssd_mixer", _ssd_init, _ssd_chunk, x, mod, w, aux,
                        [(CHUNK + SSD_HALO, SSD_CONV_DIM), (LANES, BRANCH_WIDTH)])


def _alibi_slopes(n):
    def pow2(k):
        start = 2.0 ** (-8.0 / k)
        return [start ** (i + 1) for i in range(k)]
    if math.log2(n).is_integer():
        s = pow2(n)
    else:
        c = 2 ** math.floor(math.log2(n))
        s = pow2(c) + pow2(2 * c)[0::2][: n - c]
    return [float(np.float32(v)) for v in s]


def _dil_kernel(dilation, n_back, slopes, q_ref, kc_ref, kp_ref, vc_ref, vp_ref, o_ref, lse_ref):
    blk = DIL_BLOCK
    rho = pl.program_id(2)
    log2e = 1.0 / math.log(2.0)

    def rows_of(ref, start, sl):
        if len(ref.shape) == 2:
            return ref[start:start + blk, sl]
        per = ref.shape[1]
        return jnp.concatenate([ref[t, :, sl] for t in range(start // per, (start + blk) // per)], axis=0)

    qi = _iota((blk, 2 * blk), 0)
    kj = _iota((blk, 2 * blk), 1)
    dist = qi - kj + blk
    in_window = (dist >= 0) & (dist <= n_back)
    neg_dist2 = (dist * dilation).astype(F32) * (-log2e)
    lane = _iota((blk, LANES), 1)
    for sb in range(DIL_QBLOCKS):
        if sb == 0:
            valid = in_window & (kj >= jnp.where(pl.program_id(1) > 0, 0, blk))
        else:
            valid = in_window
        bias = jnp.where(valid, neg_dist2, -jnp.inf)
        rows = pl.ds(rho + sb * blk * dilation, blk, stride=dilation) if dilation > 1 else pl.ds(sb * blk, blk)
        lse_tile = jnp.zeros((blk, LANES), F32)
        for p in range(DIL_HEADS // 2):
            sl = slice(p * LANES, (p + 1) * LANES)
            qp = rows_of(q_ref, sb * blk, sl).astype(F32) * (DIL_HEAD_DIM ** -0.5 * log2e)
            if sb == 0:
                kk = jnp.concatenate([rows_of(kp_ref, 0, sl), rows_of(kc_ref, 0, sl)], axis=0)
                vv = jnp.concatenate([rows_of(vp_ref, 0, sl), rows_of(vc_ref, 0, sl)], axis=0)
            else:
                kk = jnp.concatenate([rows_of(kc_ref, (sb - 1) * blk, sl), rows_of(kc_ref, sb * blk, sl)], axis=0)
                vv = jnp.concatenate([rows_of(vc_ref, (sb - 1) * blk, sl), rows_of(vc_ref, sb * blk, sl)], axis=0)
            pair = []
            for e in range(2):
                h = 2 * p + e
                own = (lane < DIL_HEAD_DIM) if e == 0 else (lane >= DIL_HEAD_DIM)
                qm = jnp.where(own, qp, 0.0).astype(BF16)
                s = _mm_nt(qm, kk) + slopes[h] * bias
                m = jnp.max(s, axis=-1, keepdims=True)
                pexp = jnp.exp2(s - m)
                l = jnp.sum(pexp, axis=-1, keepdims=True)
                pair.append(_mm(pexp.astype(BF16), vv) / l)
                lse_tile = jnp.where(lane == h, (m + jnp.log2(l)) * math.log(2.0), lse_tile)
            o_ref[p, rows, :] = jnp.where(lane < DIL_HEAD_DIM, pair[0], pair[1])
        lse_ref[rows, :] = lse_tile


DIL_QBLOCKS = 2


def _dilated_group(pd, g):
    nb, seq, _ = pd.shape
    window, r = DIL_PATTERNS[g]
    n = seq // r
    qrows = DIL_QBLOCKS * DIL_BLOCK
    w = BRANCH_WIDTH
    qc, kc, vc = DIL_OPERAND_COLS[g]
    if r == 1:
        view = pd
        cur = lambda col: pl.BlockSpec((None, qrows, w), lambda b, i, rho: (b, i, col))
        prev = lambda col: pl.BlockSpec((None, DIL_BLOCK, w),
                                        lambda b, i, rho: (b, jnp.maximum(DIL_QBLOCKS * i - 1, 0), col))
    else:
        per = PROJ_ROWS // r
        view = pd.reshape(nb, seq // PROJ_ROWS, r, per, 3 * w)
        cur = lambda col: pl.BlockSpec((None, qrows // per, None, per, w), lambda b, i, rho: (b, i, rho, 0, col))
        prev = lambda col: pl.BlockSpec((None, DIL_BLOCK // per, None, per, w),
                                        lambda b, i, rho: (b, jnp.maximum(DIL_QBLOCKS * i - 1, 0), rho, 0, col))
    slopes = _alibi_slopes(DIL_GROUPS * DIL_HEADS)[g * DIL_HEADS:(g + 1) * DIL_HEADS]
    pairs = DIL_HEADS // 2
    return pl.pallas_call(
        functools.partial(_dil_kernel, r, window // r, slopes),
        grid=(nb, n // qrows, r),
        in_specs=[cur(qc), cur(kc), prev(kc), cur(vc), prev(vc)],
        out_specs=[pl.BlockSpec((pairs, None, qrows * r, LANES), lambda b, i, rho: (0, b, i, 0)),
                   pl.BlockSpec((None, qrows * r, LANES), lambda b, i, rho: (b, i, 0))],
        out_shape=[jax.ShapeDtypeStruct((pairs, nb, seq, LANES), F32),
                   jax.ShapeDtypeStruct((nb, seq, LANES), F32)],
        compiler_params=_cparams("parallel", "parallel", "arbitrary"),
        name=f"dilated_attn_g{g}",
    )(view, view, view, view, view)


def _combine_dilated(lses, outs):
    m = functools.reduce(jnp.maximum, lses)
    es = [jnp.exp(l - m) for l in lses]
    den = functools.reduce(lambda a, b: a + b, es)
    e = _head_expander(DIL_HEAD_DIM)
    acc = None
    for ev, o in zip(es, outs):
        term = _expand(ev / den, e) * jnp.concatenate(o, axis=1)
        acc = term if acc is None else acc + term
    return acc


def _dilated(pds):
    nb, seq, _ = pds[0].shape
    res = [_dilated_group(pd, g) for g, pd in enumerate(pds)]
    t = nb * seq
    return [r[1].reshape(t, LANES) for r in res], [r[0].reshape(DIL_HEADS // 2, t, LANES) for r in res]


def _fused_mixer_kernel(init_fn, chunk_fn, n_aux, x0_ref, xa_ref, xb_ref, mod_ref, modn_ref, w_ref, *rest):
    aux, o_ref, p0, p1, scratch = rest[:n_aux], rest[n_aux], rest[n_aux + 1], rest[n_aux + 2], rest[n_aux + 3:]
    rows = x0_ref.shape[0]
    m = mod_ref[...]
    wrap = pl.program_id(1) == pl.num_programs(1) - 1

    def project(x_ref, p_ref, mv):
        h = jnp.concatenate([(x_ref[b] * (1.0 + mv[b, 1:2]) + mv[b, 0:1]).astype(BF16) for b in range(rows)], axis=0)
        res = _mm(h, w_ref[...])
        for b in range(rows):
            p_ref[b] = res[b * CHUNK:(b + 1) * CHUNK]

    @pl.when(pl.program_id(1) == 0)
    def _():
        init_fn(*scratch)

    @pl.when((pl.program_id(0) == 0) & (pl.program_id(1) == 0))
    def _():
        project(x0_ref, p0, m)

    project(xa_ref, p1, m)
    chunk_fn(p0, o_ref, 0, *aux, *scratch)
    project(xb_ref, p0, jnp.where(wrap, modn_ref[...], m))
    chunk_fn(p1, o_ref, 1, *aux, *scratch)


def _fused_mixer(name, init_fn, chunk_fn, x, mod, w, aux, scratch_shapes):
    nb, seq, _ = x.shape
    rows = _rows_per_step(nb)
    nc = seq // CHUNK
    n = w.shape[1]
    steps = nc // 2
    last_b = nb // rows - 1
    xblock = (rows, CHUNK, D_MODEL)

    def next_chunk(b, s):
        wrap = s == steps - 1
        return jnp.where(wrap, jnp.minimum(b + 1, last_b), b), jnp.where(wrap, 0, 2 * s + 2), 0

    return pl.pallas_call(
        functools.partial(_fused_mixer_kernel, init_fn, chunk_fn, len(aux)),
        grid=(nb // rows, steps),
        in_specs=[pl.BlockSpec(xblock, lambda b, s: (0, 0, 0)), pl.BlockSpec(xblock, lambda b, s: (b, 2 * s + 1, 0)),
                  pl.BlockSpec(xblock, next_chunk),
                  pl.BlockSpec((rows, 6, D_MODEL), lambda b, s: (b, 0, 0)),
                  pl.BlockSpec((rows, 6, D_MODEL), lambda b, s: (jnp.minimum(b + 1, last_b), 0, 0)),
                  _const_spec(w.shape)]
        + [_const_spec(a.shape) for a in aux],
        out_specs=pl.BlockSpec((rows, 2 * CHUNK, BRANCH_WIDTH), lambda b, s: (b, s, 0)),
        out_shape=jax.ShapeDtypeStruct((nb, seq, BRANCH_WIDTH), BF16),
        scratch_shapes=[pltpu.VMEM((rows, CHUNK, n), F32), pltpu.VMEM((rows, CHUNK, n), F32)]
        + [pltpu.VMEM((rows,) + s, F32) for s in scratch_shapes],
        compiler_params=_cparams("arbitrary", "arbitrary"),
        name=name,
    )(x, x, x, mod, mod, w, *aux)


def _ret_init(st):
    st[...] = jnp.zeros_like(st)


def _ret_chunk(p_ref, o_ref, half, st):
    c = CHUNK
    w = BRANCH_WIDTH
    dk = RET_DK
    scale = dk ** -0.5
    row = _iota((c, c), 0)
    col = _iota((c, c), 1)
    rel = (row - col).astype(F32)
    pos = row.astype(F32)
    lgs = [math.log(1.0 - 2.0 ** (-5.0 - h)) for h in range(RET_HEADS)]

    def pair_of(f):
        return [jnp.concatenate([f(lgs[2 * pr]), f(lgs[2 * pr + 1])], axis=1) for pr in range(RET_HEADS // 2)]

    decay = pair_of(lambda lg: jnp.where(row >= col, jnp.exp(lg * jnp.maximum(rel, 0.0)), 0.0) * scale)
    q_scale = pair_of(lambda lg: jnp.exp(lg * (pos + 1.0)))
    k_scale = pair_of(lambda lg: jnp.exp(lg * (c - 1.0 - pos)) * scale)
    on_diag = jnp.right_shift(_iota((2 * dk, 2 * dk), 0), 7) == jnp.right_shift(_iota((2 * dk, 2 * dk), 1), 7)
    upper = _iota((2 * dk, 2 * dk), 0) < dk
    for b in range(p_ref.shape[0]):
        outs = []
        for pr in range(RET_HEADS // 2):
            sl = lambda i: slice(i * w + pr * 2 * dk, i * w + (pr + 1) * 2 * dk)
            qf, kf = p_ref[b, :, sl(0)], p_ref[b, :, sl(1)]
            v16 = p_ref[b, :, sl(2)].astype(BF16)
            scores = _mm_nt(qf.astype(BF16), _block_diag(kf.astype(BF16))) * decay[pr]
            inner = _mm(scores.astype(BF16), _block_diag(v16))
            s_prev = st[b, pr]
            o = inner + _mm((qf * q_scale[pr]).astype(BF16), s_prev.astype(BF16))
            upd = _mm_tn((kf * k_scale[pr]).astype(BF16), v16)
            carry = jnp.where(upper, math.exp(lgs[2 * pr] * c), math.exp(lgs[2 * pr + 1] * c))
            st[b, pr] = s_prev * carry + jnp.where(on_diag, upd, 0.0)
            for e in range(2):
                oh = o[:, e * dk:(e + 1) * dk]
                oc = oh - jnp.mean(oh, axis=-1, keepdims=True)
                outs.append(oc * lax.rsqrt(jnp.mean(oc * oc, axis=-1, keepdims=True) + NORM_EPS))
        gate = _silu(p_ref[b, :, 3 * w:4 * w])
        o_ref[b, half * c:(half + 1) * c, :] = (jnp.concatenate(outs, axis=1) * gate).astype(o_ref.dtype)


def _retention(x, mod, w):
    return _fused_mixer("retention", _ret_init, _ret_chunk, x, mod, w, (),
                        [(RET_HEADS // 2, 2 * RET_DK, 2 * RET_DK)])


def _hgrn_init(st):
    st[...] = jnp.zeros_like(st)


def _hgrn_chunk(layer, p_ref, o_ref, half, lb_ref, nw_ref, st):
    c = CHUNK
    width = BRANCH_WIDTH
    neg_inf = -jnp.inf

    rows = [lb_ref[l:l + 1, :] for l in range(DEPTH)]
    mx = functools.reduce(jnp.maximum, rows)
    es = [jnp.exp(r - mx) for r in rows]
    den = functools.reduce(lambda a, b: a + b, es)
    sm = [e / den for e in es]
    lb = functools.reduce(lambda a, b: a + b, sm[:layer + 1]) - sm[0]

    ti = _iota((c, c), 0)
    tj = _iota((c, c), 1)
    le_t = jnp.where(tj <= ti, 1.0, 0.0)
    sizes = [c >> k for k in range(int(math.log2(c // HGRN_SUB)))]
    mats = [le_t]
    for size in sizes:
        sh = int(math.log2(size))
        last_lower = jnp.left_shift(jnp.right_shift(ti, sh), sh) + (size // 2 - 1)
        mats.append(le_t - jnp.where(tj <= last_lower, 1.0, 0.0))
    seg = jnp.concatenate(mats, axis=0).astype(BF16)
    rowi = _iota((c, width), 0)
    sub = jnp.bitwise_and(rowi, HGRN_SUB - 1)
    heads = [slice(h * HGRN_DK, (h + 1) * HGRN_DK) for h in range(HGRN_HEADS)]

    nrows = range(p_ref.shape[0])
    pre = []
    for b in nrows:
        forget = lb + (1.0 - lb) * jax.nn.sigmoid(p_ref[b, :, width:2 * width])
        lf = jnp.log(forget)
        kk = 1.0 - forget
        q = _silu(p_ref[b, :, 0:width])
        vf = p_ref[b, :, 2 * width:3 * width]
        pre.append((kk, q, vf, vf.astype(BF16), _cumsum_rows(seg, lf)))

    exact = []
    for kk, q, vf, v16, sums in pre:
        lam = sums[0:c]
        o_acc = [jnp.zeros((c, HGRN_DK), F32) for _ in heads]
        for d in range(HGRN_SUB):
            if d == 0:
                prod = q * kk
                vd = vf
            else:
                ld = pltpu.roll(lam, d, 0)
                prod = q * pltpu.roll(kk, d, 0) * jnp.exp(jnp.where(sub >= d, lam - ld, neg_inf))
                vd = pltpu.roll(vf, d, 0)
            for h, sl in enumerate(heads):
                o_acc[h] = o_acc[h] + jnp.sum(prod[:, sl], axis=-1, keepdims=True) * vd[:, sl]
        exact.append(o_acc)

    blocks = []
    for kk, q, vf, v16, sums in pre:
        attn = [jnp.zeros((c, c), F32) for _ in heads]
        for k, size in enumerate(sizes):
            mid = size // 2
            rel = sums[(k + 1) * c:(k + 2) * c]
            upper = jnp.bitwise_and(rowi, size - 1) >= mid
            z = (jnp.where(upper, q, kk) * jnp.exp(-jnp.abs(rel))).astype(BF16)
            sh = int(math.log2(size))
            pairs = ((jnp.right_shift(ti, sh) == jnp.right_shift(tj, sh))
                     & (jnp.bitwise_and(ti, size - 1) >= mid) & (jnp.bitwise_and(tj, size - 1) < mid))
            for h, sl in enumerate(heads):
                attn[h] = attn[h] + jnp.where(pairs, _mm_nt(z[:, sl], z[:, sl]), 0.0)
        blocks.append(attn)

    for b in nrows:
        kk, q, vf, v16, sums = pre[b]
        lam = sums[0:c]
        lam_last = lam[c - 1:c, :]
        q_in = (q * jnp.exp(lam)).astype(BF16)
        k_out = (kk * jnp.exp(lam_last - lam)).astype(BF16)
        e_last = jnp.exp(lam_last)
        outs = []
        for h, sl in enumerate(heads):
            s_prev = st[b, h]
            o = (exact[b][h] + _mm(blocks[b][h].astype(BF16), v16[:, sl])
                 + _mm_nt(q_in[:, sl], s_prev.astype(BF16)))
            st[b, h] = s_prev * e_last[:, sl] + _mm_tn(v16[:, sl], k_out[:, sl])
            outs.append(o * lax.rsqrt(jnp.mean(o * o, axis=-1, keepdims=True) + NORM_EPS) * nw_ref[...])
        gate = _silu(p_ref[b, :, 3 * width:4 * width])
        o_ref[b, half * c:(half + 1) * c, :] = (jnp.concatenate(outs, axis=1) * gate).astype(o_ref.dtype)


def _hgrn(x, mod, w, layer, hgrn_lb, norm_w):
    aux = (hgrn_lb.astype(F32), norm_w.reshape(1, HGRN_DK).astype(F32))
    return _fused_mixer("hgrn2", _hgrn_init, functools.partial(_hgrn_chunk, layer), x, mod, w, aux,
                        [(HGRN_HEADS, HGRN_DK, HGRN_DK)])


def _merge_kernel(x_ref, mod_ref, ssd_ref, l0_ref, l1_ref, l2_ref, d0_ref, d1_ref, d2_ref, ret_ref, hgrn_ref,
                  wm_ref, wb_ref, wo_ref, g_ref, b_ref, out_ref):
    m = mod_ref[...]
    x = x_ref[...]
    h = (x * (1.0 + m[1:2]) + m[0:1]).astype(BF16)
    pairs = range(DIL_HEADS // 2)
    dil = _combine_dilated([r[...] for r in (l0_ref, l1_ref, l2_ref)],
                           [[r[p] for p in pairs] for r in (d0_ref, d1_ref, d2_ref)]).astype(BF16)
    y = None
    for i, o in enumerate((ssd_ref[...], dil, ret_ref[...], hgrn_ref[...])):
        gate = jax.nn.sigmoid(_mm(h, wm_ref[:, i * D_MODEL:(i + 1) * D_MODEL]))
        term = gate * _mm(o, wb_ref[i])
        y = term if y is None else y + term
    mix = _mm(y.astype(BF16), wo_ref[...])
    out_ref[...] = _layer_norm_rows(DEEPNORM_ALPHA * x + m[2:3] * mix, g_ref[...], b_ref[...])


def _merge(x2, mod, o_ssd, dil, o_ret, o_hgrn, w_merge, w_branch, w_o, ln_g, ln_b, seq):
    t = x2.shape[0]
    tm = 512
    per_b = seq // tm
    lses, outs = dil
    rows = pl.BlockSpec((tm, D_MODEL), lambda i: (i, 0))
    brow = pl.BlockSpec((tm, BRANCH_WIDTH), lambda i: (i, 0))
    lrow = pl.BlockSpec((tm, LANES), lambda i: (i, 0))
    drow = pl.BlockSpec((DIL_HEADS // 2, tm, LANES), lambda i: (0, i, 0))
    return pl.pallas_call(
        _merge_kernel,
        grid=(t // tm,),
        in_specs=[rows, pl.BlockSpec((None, 6, D_MODEL), lambda i: (i // per_b, 0, 0)), brow]
        + [lrow] * DIL_GROUPS + [drow] * DIL_GROUPS + [brow, brow]
        + [_const_spec(w_merge.shape), _const_spec(w_branch.shape), _const_spec(w_o.shape),
           _const_spec((1, D_MODEL)), _const_spec((1, D_MODEL))],
        out_specs=rows,
        out_shape=jax.ShapeDtypeStruct((t, D_MODEL), F32),
        compiler_params=_cparams("parallel"),
        name="merge_out_ln",
    )(x2, mod, o_ssd, *lses, *outs, o_ret, o_hgrn, w_merge, w_branch, w_o, ln_g.reshape(1, -1), ln_b.reshape(1, -1))


def _ffn_kernel(x_ref, mod_ref, wu_ref, cw_ref, cb_ref, wd_ref, g_ref, b_ref, out_ref, ubuf0, ubuf1, tail, act):
    tm = x_ref.shape[0]
    halo = 8
    assert N_FF_CHUNKS % 2 == 1

    @pl.when(pl.program_id(1) == 0)
    def _():
        tail[...] = jnp.zeros_like(tail)

    m = mod_ref[...]
    x = x_ref[...]
    h = (x * (1.0 + m[4:5]) + m[3:4]).astype(BF16)

    def cols(ref, j):
        return [ref[:, part * D_FF + j * FF_CHUNK:part * D_FF + (j + 1) * FF_CHUNK] for part in range(2)]

    def up(j, ubuf):
        ubuf[0:halo, :] = tail[j]
        for part, w in enumerate(cols(wu_ref, j)):
            ubuf[halo:halo + tm, part * FF_CHUNK:(part + 1) * FF_CHUNK] = _mm(h, w)
        tail[j] = ubuf[tm:tm + halo, :]

    def gate(j, ubuf):
        cw = jnp.concatenate(cols(cw_ref, j), axis=1)
        u = jnp.broadcast_to(jnp.concatenate(cols(cb_ref, j), axis=1), (tm, 2 * FF_CHUNK))
        for k in range(FFN_CONV):
            off = halo - (FFN_CONV - 1) + k
            u = u + cw[k:k + 1, :] * ubuf[off:off + tm, :]
        act[j] = (_silu(u[:, :FF_CHUNK]) * u[:, FF_CHUNK:]).astype(BF16)

    bufs = (ubuf0, ubuf1)
    up(0, ubuf0)
    for j in range(N_FF_CHUNKS):
        if j + 1 < N_FF_CHUNKS:
            up(j + 1, bufs[(j + 1) % 2])
        gate(j, bufs[j % 2])
    ffn = _mm(act[0], wd_ref[0])
    for j in range(1, N_FF_CHUNKS):
        ffn = ffn + _mm(act[j], wd_ref[j])
    out_ref[...] = _layer_norm_rows(DEEPNORM_ALPHA * x + m[5:6] * ffn, g_ref[...], b_ref[...])


def _ffn(x2, mod, w_up, conv_w, conv_b, w_down, ln_g, ln_b, nb, seq):
    tm = 512
    per_b = seq // tm
    rows = pl.BlockSpec((tm, D_MODEL), lambda b, i: (b * per_b + i, 0))
    return pl.pallas_call(
        _ffn_kernel,
        grid=(nb, per_b),
        in_specs=[rows, pl.BlockSpec((None, 6, D_MODEL), lambda b, i: (b, 0, 0)),
                  _const_spec(w_up.shape), _const_spec(conv_w.shape), _const_spec(conv_b.shape),
                  _const_spec(w_down.shape), _const_spec((1, D_MODEL)), _const_spec((1, D_MODEL))],
        out_specs=rows,
        out_shape=jax.ShapeDtypeStruct((nb * seq, D_MODEL), F32),
        scratch_shapes=[pltpu.VMEM((tm + 8, 2 * FF_CHUNK), F32),
                        pltpu.VMEM((tm + 8, 2 * FF_CHUNK), F32),
                        pltpu.VMEM((N_FF_CHUNKS, 8, 2 * FF_CHUNK), F32),
                        pltpu.VMEM((N_FF_CHUNKS, tm, FF_CHUNK), BF16)],
        compiler_params=_cparams("parallel", "arbitrary"),
        name="conv_ffn_ln",
    )(x2, mod, w_up, conv_w, conv_b, w_down, ln_g.reshape(1, -1), ln_b.reshape(1, -1))


def _pack_in_proj(w_in):
    offs = np.concatenate([[0], np.cumsum(SECTION_SIZES)])
    sec = {n: w_in[:, offs[i]:offs[i + 1]] for i, n in enumerate(SECTION_NAMES)}
    w = BRANCH_WIDTH
    sec.update({f"q{g}": sec["dq"][:, g * w:(g + 1) * w] for g in range(DIL_GROUPS)}, k=sec["dk"], v=sec["dv"])
    sec["dt_pad"] = jnp.zeros((D_MODEL, LANES - SSD_HEADS), w_in.dtype)
    cat = lambda names: jnp.concatenate([sec[n] for n in names], axis=1).astype(BF16)
    return {"dil": cat(sorted(PD_COLS, key=PD_COLS.get)), "ssd": cat(("z", "xbc", "dt", "dt_pad")),
            "ret": cat(("rq", "rk", "rv", "rg")), "hgrn": cat(("hq", "hf", "hi", "hg"))}


def kernel(x, c, w_ada, b_ada, w_in, w_merge, ssd_conv_w, ssd_conv_b, ssd_dt_bias, ssd_a_log, ssd_d, ssd_norm_w, hgrn_lb, hgrn_norm_w, w_branch_out, w_o, ln1_g, ln1_b, w_up, ffn_conv_w, ffn_conv_b, w_down, ln2_g, ln2_b):
    nb, seq, _ = x.shape
    assert seq % (DIL_PATTERNS[-1][1] * DIL_QBLOCKS * DIL_BLOCK) == 0 and x.shape[-1] == D_MODEL
    mods = _modulation(c, w_ada, b_ada).reshape(DEPTH, nb, 6, D_MODEL)
    x2 = x.reshape(nb * seq, D_MODEL)
    for layer in range(DEPTH):
        mod = mods[layer]
        w_proj = _pack_in_proj(w_in[layer])
        x3 = x2.reshape(nb, seq, D_MODEL)
        o_dil = _dilated([a.reshape(nb, seq, -1) for a in _project_dilated(x2, mod, w_proj["dil"], seq)])
        o_ssd = _ssd(x3, mod, w_proj["ssd"], ssd_conv_w[layer], ssd_conv_b[layer], ssd_dt_bias[layer],
                     ssd_a_log[layer], ssd_d[layer], ssd_norm_w[layer])
        o_ret = _retention(x3, mod, w_proj["ret"])
        o_hgrn = _hgrn(x3, mod, w_proj["hgrn"], layer, hgrn_lb, hgrn_norm_w[layer])
        flat = lambda a: a.reshape(nb * seq, BRANCH_WIDTH)
        x2 = _merge(x2, mod, flat(o_ssd), o_dil, flat(o_ret), flat(o_hgrn), w_merge[layer].astype(BF16),
                    w_branch_out[layer].astype(BF16), w_o[layer].astype(BF16), ln1_g[layer], ln1_b[layer], seq)
        x2 = _ffn(x2, mod, w_up[layer].astype(BF16), ffn_conv_w[layer], ffn_conv_b[layer].reshape(1, -1),
                  w_down[layer].reshape(N_FF_CHUNKS, FF_CHUNK, D_MODEL).astype(BF16),
                  ln2_g[layer], ln2_b[layer], nb, seq)
    return x2.reshape(nb, seq, D_MODEL)
```

```python
import functools
import math

import numpy as np
import jax
import jax.numpy as jnp
from jax import lax
from jax.experimental import pallas as pl
from jax.experimental.pallas import tpu as pltpu

F32 = jnp.float32
BF16 = jnp.bfloat16

D_MODEL = 1024
DEPTH = 2
DEEPNORM_ALPHA = (2 * DEPTH) ** 0.25
NORM_EPS = 1e-5

BRANCH_WIDTH = D_MODEL // 2

SSD_HEAD_DIM = 64
SSD_HEADS = BRANCH_WIDTH // SSD_HEAD_DIM
SSD_GROUPS = 2
SSD_STATE = 64
SSD_CONV = 4
SSD_CONV_DIM = BRANCH_WIDTH + 2 * SSD_GROUPS * SSD_STATE

DIL_HEAD_DIM = 64
DIL_HEADS = BRANCH_WIDTH // DIL_HEAD_DIM
DIL_PATTERNS = ((128, 1), (512, 4), (2048, 16))
DIL_GROUPS = len(DIL_PATTERNS)
DIL_BLOCK = 128

RET_HEADS = 4
RET_DK = BRANCH_WIDTH // RET_HEADS

HGRN_HEADS = 4
HGRN_DK = BRANCH_WIDTH // HGRN_HEADS
HGRN_SUB = 4

D_FF = 256 * ((8 * D_MODEL // 3 + 255) // 256)
FFN_CONV = 3
FF_CHUNK = 256
N_FF_CHUNKS = D_FF // FF_CHUNK

SECTION_SIZES = (
    BRANCH_WIDTH, SSD_CONV_DIM, SSD_HEADS,
    DIL_GROUPS * BRANCH_WIDTH, BRANCH_WIDTH, BRANCH_WIDTH,
    BRANCH_WIDTH, BRANCH_WIDTH, BRANCH_WIDTH, BRANCH_WIDTH,
    BRANCH_WIDTH, BRANCH_WIDTH, BRANCH_WIDTH, BRANCH_WIDTH,
)
SECTION_NAMES = ("z", "xbc", "dt", "dq", "dk", "dv", "rq", "rk", "rv", "rg", "hq", "hf", "hi", "hg")

LANES = 128
SUBLANES = 8
V7X_VMEM_BYTES = 64 * 1024 * 1024
VMEM_LIMIT = V7X_VMEM_BYTES - 8 * 1024 * 1024
CHUNK = 128
PD_COLS = {n: i for i, n in enumerate(("q0", "q1", "k", "v", "q2"))}
MIXER_ROWS = 2


def _log2(n):
    return int(math.log2(n))


def _rows_per_step(nb):
    return MIXER_ROWS if nb % MIXER_ROWS == 0 else 1


def _cparams(*sem):
    return pltpu.CompilerParams(dimension_semantics=sem, vmem_limit_bytes=VMEM_LIMIT)


def _silu(v):
    return v * jax.nn.sigmoid(v)


def _mm(a, b):
    return jnp.dot(a, b, preferred_element_type=F32)


def _mm_nt(a, b):
    return lax.dot_general(a, b, (((1,), (1,)), ((), ())), preferred_element_type=F32)


def _mm_tn(a, b):
    return lax.dot_general(a, b, (((0,), (0,)), ((), ())), preferred_element_type=F32)


def _split2(a):
    hi = a.astype(BF16)
    lo = (a - hi.astype(F32)).astype(BF16)
    return hi, lo


def _split3(a):
    hi = a.astype(BF16)
    r = a - hi.astype(F32)
    mid = r.astype(BF16)
    lo = (r - mid.astype(F32)).astype(BF16)
    return hi, mid, lo


def _mm_f32(a, b):
    ah, al = _split2(a)
    bh, bl = _split2(b)
    return _mm(ah, bh) + (_mm(ah, bl) + _mm(al, bh))


def _cumsum_rows(tri, a, lo_rows=None):
    hi, mid, lo = _split3(a)
    main = _mm(jnp.concatenate([tri, tri], axis=1), jnp.concatenate([hi, mid], axis=0))
    if lo_rows is None:
        return main + _mm(tri, lo)
    return jnp.concatenate([main[:lo_rows] + _mm(tri[:lo_rows], lo), main[lo_rows:]], axis=0)


def _expand(a, e):
    hi, lo = _split2(a)
    return _mm(jnp.concatenate([hi, lo], axis=1), jnp.concatenate([e, e], axis=0))


def _block_diag(a):
    d = a.shape[1] // 2
    zero = jnp.zeros((a.shape[0], d), a.dtype)
    return jnp.concatenate([jnp.concatenate([a[:, :d], zero], axis=1),
                            jnp.concatenate([zero, a[:, d:]], axis=1)], axis=0)


def _iota(shape, dim):
    return lax.broadcasted_iota(jnp.int32, shape, dim)


def _tri(n):
    return jnp.where(_iota((n, n), 1) <= _iota((n, n), 0), 1.0, 0.0).astype(BF16)


def _head_expander(width):
    shape = (LANES, BRANCH_WIDTH)
    shift = _log2(width)
    return jnp.where(jnp.right_shift(_iota(shape, 1), shift) == _iota(shape, 0), 1.0, 0.0).astype(BF16)


def _layer_norm_rows(v, g, b):
    vc = v - jnp.mean(v, axis=-1, keepdims=True)
    return vc * lax.rsqrt(jnp.mean(vc * vc, axis=-1, keepdims=True) + NORM_EPS) * g + b


def _mod_kernel(c_ref, w_ref, b_ref, o_ref):
    o_ref[0] = _mm_f32(_silu(c_ref[...]), w_ref[0]) + b_ref[0]


def _modulation(c, w_ada, b_ada):
    nb = c.shape[0]
    tn = 1536
    return pl.pallas_call(
        _mod_kernel,
        grid=(DEPTH, 6 * D_MODEL // tn),
        in_specs=[pl.BlockSpec((nb, D_MODEL), lambda l, j: (0, 0)),
                  pl.BlockSpec((1, D_MODEL, tn), lambda l, j: (l, 0, j)),
                  pl.BlockSpec((1, 1, tn), lambda l, j: (l, 0, j))],
        out_specs=pl.BlockSpec((1, nb, tn), lambda l, j: (l, 0, j)),
        out_shape=jax.ShapeDtypeStruct((DEPTH, nb, 6 * D_MODEL), F32),
        compiler_params=_cparams("parallel", "parallel"),
        name="adaln_mod",
    )(c, w_ada, b_ada.reshape(DEPTH, 1, 6 * D_MODEL))


def _const_spec(shape):
    zeros = (0,) * len(shape)
    return pl.BlockSpec(shape, lambda *_: zeros, pipeline_mode=pl.Buffered(1))


def _destride_matrix(n, r):
    per = n // r
    i = _iota((n, n), 0)
    src = jnp.left_shift(jnp.bitwise_and(i, per - 1), _log2(r)) + jnp.right_shift(i, _log2(per))
    return jnp.where(_iota((n, n), 1) == src, 1.0, 0.0).astype(BF16)


def _proj_kernel(x_ref, mod_ref, wd_ref, od0_ref, od1_ref, od2_ref):
    m = mod_ref[...]
    h = (x_ref[...] * (1.0 + m[1:2]) + m[0:1]).astype(BF16)
    d = _mm(h, wd_ref[...]).astype(BF16)
    w = BRANCH_WIDTH
    tm = d.shape[0]
    od0_ref[...] = jnp.concatenate([d[:, 0:w], d[:, 2 * w:4 * w]], axis=1)
    od1_ref[...] = _mm(_destride_matrix(tm, DIL_PATTERNS[1][1]), d[:, w:4 * w]).astype(BF16)
    od2_ref[...] = _mm(_destride_matrix(tm, DIL_PATTERNS[2][1]), d[:, 2 * w:5 * w]).astype(BF16)


PROJ_ROWS = 256
DIL_OPERAND_COLS = ((0, 1, 2), (0, 1, 2), (2, 0, 1))


def _project_dilated(x2, mod, wd, seq):
    t = x2.shape[0]
    tm = PROJ_ROWS
    per_b = seq // tm
    qkv = 3 * BRANCH_WIDTH
    return pl.pallas_call(
        _proj_kernel,
        grid=(t // tm,),
        in_specs=[pl.BlockSpec((tm, D_MODEL), lambda i: (i, 0)),
                  pl.BlockSpec((None, 6, D_MODEL), lambda i: (i // per_b, 0, 0)),
                  _const_spec(wd.shape)],
        out_specs=[pl.BlockSpec((tm, qkv), lambda i: (i, 0))] * DIL_GROUPS,
        out_shape=[jax.ShapeDtypeStruct((t, qkv), BF16)] * DIL_GROUPS,
        compiler_params=_cparams("parallel"),
        name="in_proj_dilated",
    )(x2, mod, wd)


SSD_HALO = SUBLANES


def _ssd_init(xbuf, st):
    xbuf[:, 0:SSD_HALO, :] = jnp.zeros((xbuf.shape[0], SSD_HALO, SSD_CONV_DIM), F32)
    st[...] = jnp.zeros_like(st)


def _ssd_chunk(p_ref, o_ref, half, cw_ref, cb_ref, dtb_ref, alog_ref, dsk_ref, nw_ref, xbuf, st):
    c = CHUNK
    halo = SSD_HALO
    xbc0 = BRANCH_WIDTH
    dt0 = BRANCH_WIDTH + SSD_CONV_DIM
    for b in range(p_ref.shape[0]):
        xbuf[b, halo:halo + c, :] = p_ref[b, :, xbc0:dt0]
        acc = jnp.broadcast_to(cb_ref[...], (c, SSD_CONV_DIM))
        for k in range(SSD_CONV):
            off = halo - (SSD_CONV - 1) + k
            acc = acc + cw_ref[k:k + 1, :] * xbuf[b, off:off + c, :]
        xbuf[b, 0:halo, :] = xbuf[b, c:c + halo, :]
        y = _silu(acc)
        xs = y[:, :BRANCH_WIDTH]
        bm = y[:, BRANCH_WIDTH:BRANCH_WIDTH + LANES]
        cm = y[:, BRANCH_WIDTH + LANES:]

        dtr = p_ref[b, :, dt0:dt0 + LANES] + dtb_ref[...]
        dt = jnp.maximum(dtr, 0.0) + jnp.log1p(jnp.exp(-jnp.abs(dtr)))
        da = dt * (-jnp.exp(alog_ref[...]))
        cs = _cumsum_rows(_tri(c), da)
        cs_t = cs.T
        tot = cs[c - 1:c, :]

        stack = jnp.concatenate([dt, jnp.exp(tot - cs), jnp.exp(cs), jnp.broadcast_to(jnp.exp(tot), (8, LANES))], axis=0)
        ex = _expand(stack, _head_expander(SSD_HEAD_DIM))
        dt_e, ds_e, ecs_e, dec_e = ex[0:c], ex[c:2 * c], ex[2 * c:3 * c], ex[3 * c:3 * c + 1]

        xdt = xs * dt_e
        xds = xdt * ds_e
        causal = _iota((c, c), 1) <= _iota((c, c), 0)
        lane = _iota((c, LANES), 1)
        bm16 = bm.astype(BF16)
        cbs = []
        for g in range(SSD_GROUPS):
            cm_g = jnp.where(jnp.right_shift(lane, _log2(SSD_STATE)) == g, cm, 0.0).astype(BF16)
            cbs.append(_mm_nt(cm_g, bm16))
        parts = []
        for p in range(SSD_HEADS // 2):
            g = (2 * p) // (SSD_HEADS // SSD_GROUPS)
            ms = []
            for e in range(2):
                h = 2 * p + e
                diff = cs[:, h:h + 1] - cs_t[h:h + 1, :]
                ms.append((cbs[g] * jnp.exp(jnp.where(causal, diff, -jnp.inf))).astype(BF16))
            xp = xdt[:, p * LANES:(p + 1) * LANES]
            xbd = jnp.concatenate([jnp.where(lane < SSD_HEAD_DIM, xp, 0.0), jnp.where(lane >= SSD_HEAD_DIM, xp, 0.0)],
                                  axis=0).astype(BF16)
            parts.append(_mm(jnp.concatenate(ms, axis=1), xbd))
        y_diag = jnp.concatenate(parts, axis=1)

        s_prev = st[b]
        y_off = _mm(cm.astype(BF16), s_prev.astype(BF16)) * ecs_e
        upd = _mm(bm.T.astype(BF16), xds.astype(BF16))
        shape = (LANES, BRANCH_WIDTH)
        same_group = (jnp.right_shift(_iota(shape, 0), _log2(SSD_STATE))
                      == jnp.right_shift(_iota(shape, 1), _log2(BRANCH_WIDTH // SSD_GROUPS)))
        st[b] = s_prev * dec_e + jnp.where(same_group, upd, 0.0)

        yv = (y_diag + y_off + xs * dsk_ref[...]) * _silu(p_ref[b, :, 0:BRANCH_WIDTH])
        gw = BRANCH_WIDTH // SSD_GROUPS
        outs = []
        for g in range(SSD_GROUPS):
            yg = yv[:, g * gw:(g + 1) * gw]
            outs.append(yg * lax.rsqrt(jnp.mean(yg * yg, axis=-1, keepdims=True) + NORM_EPS))
        o_ref[b, half * c:(half + 1) * c, :] = (jnp.concatenate(outs, axis=1) * nw_ref[...]).astype(o_ref.dtype)


def _ssd(x, mod, w, conv_w, conv_b, dt_bias, a_log, d_skip, norm_w):
    pad = LANES - SSD_HEADS
    row = lambda v: v.reshape(1, -1).astype(F32)
    aux = (conv_w.astype(F32), row(conv_b), row(jnp.pad(dt_bias, (0, pad))), row(jnp.pad(a_log, (0, pad))),
           row(jnp.repeat(d_skip, SSD_HEAD_DIM)), row(norm_w))
    return _fused_mixer("ssd_mixer", _ssd_init, _ssd_chunk, x, mod, w, aux,
                        [(CHUNK + SSD_HALO, SSD_CONV_DIM), (LANES, BRANCH_WIDTH)])


def _alibi_slopes(n):
    def pow2(k):
        start = 2.0 ** (-8.0 / k)
        return [start ** (i + 1) for i in range(k)]
    if math.log2(n).is_integer():
        s = pow2(n)
    else:
        c = 2 ** math.floor(math.log2(n))
        s = pow2(c) + pow2(2 * c)[0::2][: n - c]
    return [float(np.float32(v)) for v in s]


def _dil_kernel(dilation, n_back, slopes, q_ref, kc_ref, kp_ref, vc_ref, vp_ref, o_ref, lse_ref):
    blk = DIL_BLOCK
    rho = pl.program_id(2)
    log2e = 1.0 / math.log(2.0)

    def rows_of(ref, start, sl):
        if len(ref.shape) == 2:
            return ref[start:start + blk, sl]
        per = ref.shape[1]
        return jnp.concatenate([ref[t, :, sl] for t in range(start // per, (start + blk) // per)], axis=0)

    qi = _iota((blk, 2 * blk), 0)
    kj = _iota((blk, 2 * blk), 1)
    dist = qi - kj + blk
    in_window = (dist >= 0) & (dist <= n_back)
    neg_dist2 = (dist * dilation).astype(F32) * (-log2e)
    lane = _iota((blk, LANES), 1)
    for sb in range(DIL_QBLOCKS):
        if sb == 0:
            valid = in_window & (kj >= jnp.where(pl.program_id(1) > 0, 0, blk))
        else:
            valid = in_window
        bias = jnp.where(valid, neg_dist2, -jnp.inf)
        rows = pl.ds(rho + sb * blk * dilation, blk, stride=dilation) if dilation > 1 else pl.ds(sb * blk, blk)
        lse_tile = jnp.zeros((blk, LANES), F32)
        for p in range(DIL_HEADS // 2):
            sl = slice(p * LANES, (p + 1) * LANES)
            qp = rows_of(q_ref, sb * blk, sl).astype(F32) * (DIL_HEAD_DIM ** -0.5 * log2e)
            if sb == 0:
                kk = jnp.concatenate([rows_of(kp_ref, 0, sl), rows_of(kc_ref, 0, sl)], axis=0)
                vv = jnp.concatenate([rows_of(vp_ref, 0, sl), rows_of(vc_ref, 0, sl)], axis=0)
            else:
                kk = jnp.concatenate([rows_of(kc_ref, (sb - 1) * blk, sl), rows_of(kc_ref, sb * blk, sl)], axis=0)
                vv = jnp.concatenate([rows_of(vc_ref, (sb - 1) * blk, sl), rows_of(vc_ref, sb * blk, sl)], axis=0)
            pair = []
            for e in range(2):
                h = 2 * p + e
                own = (lane < DIL_HEAD_DIM) if e == 0 else (lane >= DIL_HEAD_DIM)
                qm = jnp.where(own, qp, 0.0).astype(BF16)
                s = _mm_nt(qm, kk) + slopes[h] * bias
                m = jnp.max(s, axis=-1, keepdims=True)
                pexp = jnp.exp2(s - m)
                l = jnp.sum(pexp, axis=-1, keepdims=True)
                pair.append(_mm(pexp.astype(BF16), vv) / l)
                lse_tile = jnp.where(lane == h, (m + jnp.log2(l)) * math.log(2.0), lse_tile)
            o_ref[p, rows, :] = jnp.where(lane < DIL_HEAD_DIM, pair[0], pair[1])
        lse_ref[rows, :] = lse_tile


DIL_QBLOCKS = 2


def _dilated_group(pd, g):
    nb, seq, _ = pd.shape
    window, r = DIL_PATTERNS[g]
    n = seq // r
    qrows = DIL_QBLOCKS * DIL_BLOCK
    w = BRANCH_WIDTH
    qc, kc, vc = DIL_OPERAND_COLS[g]
    if r == 1:
        view = pd
        cur = lambda col: pl.BlockSpec((None, qrows, w), lambda b, i, rho: (b, i, col))
        prev = lambda col: pl.BlockSpec((None, DIL_BLOCK, w),
                                        lambda b, i, rho: (b, jnp.maximum(DIL_QBLOCKS * i - 1, 0), col))
    else:
        per = PROJ_ROWS // r
        view = pd.reshape(nb, seq // PROJ_ROWS, r, per, 3 * w)
        cur = lambda col: pl.BlockSpec((None, qrows // per, None, per, w), lambda b, i, rho: (b, i, rho, 0, col))
        prev = lambda col: pl.BlockSpec((None, DIL_BLOCK // per, None, per, w),
                                        lambda b, i, rho: (b, jnp.maximum(DIL_QBLOCKS * i - 1, 0), rho, 0, col))
    slopes = _alibi_slopes(DIL_GROUPS * DIL_HEADS)[g * DIL_HEADS:(g + 1) * DIL_HEADS]
    pairs = DIL_HEADS // 2
    return pl.pallas_call(
        functools.partial(_dil_kernel, r, window // r, slopes),
        grid=(nb, n // qrows, r),
        in_specs=[cur(qc), cur(kc), prev(kc), cur(vc), prev(vc)],
        out_specs=[pl.BlockSpec((pairs, None, qrows * r, LANES), lambda b, i, rho: (0, b, i, 0)),
                   pl.BlockSpec((None, qrows * r, LANES), lambda b, i, rho: (b, i, 0))],
        out_shape=[jax.ShapeDtypeStruct((pairs, nb, seq, LANES), F32),
                   jax.ShapeDtypeStruct((nb, seq, LANES), F32)],
        compiler_params=_cparams("parallel", "parallel", "arbitrary"),
        name=f"dilated_attn_g{g}",
    )(view, view, view, view, view)


def _combine_dilated(lses, outs):
    m = functools.reduce(jnp.maximum, lses)
    es = [jnp.exp(l - m) for l in lses]
    den = functools.reduce(lambda a, b: a + b, es)
    e = _head_expander(DIL_HEAD_DIM)
    acc = None
    for ev, o in zip(es, outs):
        term = _expand(ev / den, e) * jnp.concatenate(o, axis=1)
        acc = term if acc is None else acc + term
    return acc


def _dilated(pds):
    nb, seq, _ = pds[0].shape
    res = [_dilated_group(pd, g) for g, pd in enumerate(pds)]
    t = nb * seq
    return [r[1].reshape(t, LANES) for r in res], [r[0].reshape(DIL_HEADS // 2, t, LANES) for r in res]


def _fused_mixer_kernel(init_fn, chunk_fn, n_aux, x0_ref, xa_ref, xb_ref, mod_ref, modn_ref, w_ref, *rest):
    aux, o_ref, p0, p1, scratch = rest[:n_aux], rest[n_aux], rest[n_aux + 1], rest[n_aux + 2], rest[n_aux + 3:]
    rows = x0_ref.shape[0]
    m = mod_ref[...]
    wrap = pl.program_id(1) == pl.num_programs(1) - 1

    def project(x_ref, p_ref, mv):
        h = jnp.concatenate([(x_ref[b] * (1.0 + mv[b, 1:2]) + mv[b, 0:1]).astype(BF16) for b in range(rows)], axis=0)
        res = _mm(h, w_ref[...])
        for b in range(rows):
            p_ref[b] = res[b * CHUNK:(b + 1) * CHUNK]

    @pl.when(pl.program_id(1) == 0)
    def _():
        init_fn(*scratch)

    @pl.when((pl.program_id(0) == 0) & (pl.program_id(1) == 0))
    def _():
        project(x0_ref, p0, m)

    project(xa_ref, p1, m)
    chunk_fn(p0, o_ref, 0, *aux, *scratch)
    project(xb_ref, p0, jnp.where(wrap, modn_ref[...], m))
    chunk_fn(p1, o_ref, 1, *aux, *scratch)


def _fused_mixer(name, init_fn, chunk_fn, x, mod, w, aux, scratch_shapes):
    nb, seq, _ = x.shape
    rows = _rows_per_step(nb)
    nc = seq // CHUNK
    n = w.shape[1]
    steps = nc // 2
    last_b = nb // rows - 1
    xblock = (rows, CHUNK, D_MODEL)

    def next_chunk(b, s):
        wrap = s == steps - 1
        return jnp.where(wrap, jnp.minimum(b + 1, last_b), b), jnp.where(wrap, 0, 2 * s + 2), 0

    return pl.pallas_call(
        functools.partial(_fused_mixer_kernel, init_fn, chunk_fn, len(aux)),
        grid=(nb // rows, steps),
        in_specs=[pl.BlockSpec(xblock, lambda b, s: (0, 0, 0)), pl.BlockSpec(xblock, lambda b, s: (b, 2 * s + 1, 0)),
                  pl.BlockSpec(xblock, next_chunk),
                  pl.BlockSpec((rows, 6, D_MODEL), lambda b, s: (b, 0, 0)),
                  pl.BlockSpec((rows, 6, D_MODEL), lambda b, s: (jnp.minimum(b + 1, last_b), 0, 0)),
                  _const_spec(w.shape)]
        + [_const_spec(a.shape) for a in aux],
        out_specs=pl.BlockSpec((rows, 2 * CHUNK, BRANCH_WIDTH), lambda b, s: (b, s, 0)),
        out_shape=jax.ShapeDtypeStruct((nb, seq, BRANCH_WIDTH), BF16),
        scratch_shapes=[pltpu.VMEM((rows, CHUNK, n), F32), pltpu.VMEM((rows, CHUNK, n), F32)]
        + [pltpu.VMEM((rows,) + s, F32) for s in scratch_shapes],
        compiler_params=_cparams("arbitrary", "arbitrary"),
        name=name,
    )(x, x, x, mod, mod, w, *aux)


def _ret_init(st):
    st[...] = jnp.zeros_like(st)


def _ret_chunk(p_ref, o_ref, half, st):
    c = CHUNK
    w = BRANCH_WIDTH
    dk = RET_DK
    scale = dk ** -0.5
    row = _iota((c, c), 0)
    col = _iota((c, c), 1)
    rel = (row - col).astype(F32)
    pos = row.astype(F32)
    lgs = [math.log(1.0 - 2.0 ** (-5.0 - h)) for h in range(RET_HEADS)]

    def pair_of(f):
        return [jnp.concatenate([f(lgs[2 * pr]), f(lgs[2 * pr + 1])], axis=1) for pr in range(RET_HEADS // 2)]

    decay = pair_of(lambda lg: jnp.where(row >= col, jnp.exp(lg * jnp.maximum(rel, 0.0)), 0.0) * scale)
    q_scale = pair_of(lambda lg: jnp.exp(lg * (pos + 1.0)))
    k_scale = pair_of(lambda lg: jnp.exp(lg * (c - 1.0 - pos)) * scale)
    on_diag = (jnp.right_shift(_iota((2 * dk, 2 * dk), 0), _log2(dk))
               == jnp.right_shift(_iota((2 * dk, 2 * dk), 1), _log2(dk)))
    upper = _iota((2 * dk, 2 * dk), 0) < dk
    for b in range(p_ref.shape[0]):
        outs = []
        for pr in range(RET_HEADS // 2):
            sl = lambda i: slice(i * w + pr * 2 * dk, i * w + (pr + 1) * 2 * dk)
            qf, kf = p_ref[b, :, sl(0)], p_ref[b, :, sl(1)]
            v16 = p_ref[b, :, sl(2)].astype(BF16)
            scores = _mm_nt(qf.astype(BF16), _block_diag(kf.astype(BF16))) * decay[pr]
            inner = _mm(scores.astype(BF16), _block_diag(v16))
            s_prev = st[b, pr]
            o = inner + _mm((qf * q_scale[pr]).astype(BF16), s_prev.astype(BF16))
            upd = _mm_tn((kf * k_scale[pr]).astype(BF16), v16)
            carry = jnp.where(upper, math.exp(lgs[2 * pr] * c), math.exp(lgs[2 * pr + 1] * c))
            st[b, pr] = s_prev * carry + jnp.where(on_diag, upd, 0.0)
            for e in range(2):
                oh = o[:, e * dk:(e + 1) * dk]
                oc = oh - jnp.mean(oh, axis=-1, keepdims=True)
                outs.append(oc * lax.rsqrt(jnp.mean(oc * oc, axis=-1, keepdims=True) + NORM_EPS))
        gate = _silu(p_ref[b, :, 3 * w:4 * w])
        o_ref[b, half * c:(half + 1) * c, :] = (jnp.concatenate(outs, axis=1) * gate).astype(o_ref.dtype)


def _retention(x, mod, w):
    return _fused_mixer("retention", _ret_init, _ret_chunk, x, mod, w, (),
                        [(RET_HEADS // 2, 2 * RET_DK, 2 * RET_DK)])


def _hgrn_init(st):
    st[...] = jnp.zeros_like(st)


def _hgrn_chunk(layer, p_ref, o_ref, half, lb_ref, nw_ref, st):
    c = CHUNK
    width = BRANCH_WIDTH
    neg_inf = -jnp.inf

    rows = [lb_ref[l:l + 1, :] for l in range(DEPTH)]
    mx = functools.reduce(jnp.maximum, rows)
    es = [jnp.exp(r - mx) for r in rows]
    den = functools.reduce(lambda a, b: a + b, es)
    sm = [e / den for e in es]
    lb = functools.reduce(lambda a, b: a + b, sm[:layer + 1]) - sm[0]

    ti = _iota((c, c), 0)
    tj = _iota((c, c), 1)
    le_t = jnp.where(tj <= ti, 1.0, 0.0)
    sizes = [c >> k for k in range(_log2(c // HGRN_SUB))]
    mats = [le_t]
    for size in sizes:
        sh = _log2(size)
        last_lower = jnp.left_shift(jnp.right_shift(ti, sh), sh) + (size // 2 - 1)
        mats.append(le_t - jnp.where(tj <= last_lower, 1.0, 0.0))
    seg = jnp.concatenate(mats, axis=0).astype(BF16)
    rowi = _iota((c, width), 0)
    sub = jnp.bitwise_and(rowi, HGRN_SUB - 1)
    heads = [slice(h * HGRN_DK, (h + 1) * HGRN_DK) for h in range(HGRN_HEADS)]

    nrows = range(p_ref.shape[0])
    pre = []
    for b in nrows:
        forget = lb + (1.0 - lb) * jax.nn.sigmoid(p_ref[b, :, width:2 * width])
        lf = jnp.log(forget)
        kk = 1.0 - forget
        q = _silu(p_ref[b, :, 0:width])
        vf = p_ref[b, :, 2 * width:3 * width]
        pre.append((kk, q, vf, vf.astype(BF16), _cumsum_rows(seg, lf, lo_rows=c)))

    exact = []
    for kk, q, vf, v16, sums in pre:
        lam = sums[0:c]
        o_acc = [jnp.zeros((c, HGRN_DK), F32) for _ in heads]
        for d in range(HGRN_SUB):
            if d == 0:
                prod = q * kk
                vd = vf
            else:
                ld = pltpu.roll(lam, d, 0)
                prod = q * pltpu.roll(kk, d, 0) * jnp.exp(jnp.where(sub >= d, lam - ld, neg_inf))
                vd = pltpu.roll(vf, d, 0)
            for h, sl in enumerate(heads):
                o_acc[h] = o_acc[h] + jnp.sum(prod[:, sl], axis=-1, keepdims=True) * vd[:, sl]
        exact.append(o_acc)

    blocks = []
    for kk, q, vf, v16, sums in pre:
        attn = [jnp.zeros((c, c), F32) for _ in heads]
        for k, size in enumerate(sizes):
            mid = size // 2
            rel = sums[(k + 1) * c:(k + 2) * c]
            upper = jnp.bitwise_and(rowi, size - 1) >= mid
            z = (jnp.where(upper, q, kk) * jnp.exp(-jnp.abs(rel))).astype(BF16)
            sh = _log2(size)
            pairs = ((jnp.right_shift(ti, sh) == jnp.right_shift(tj, sh))
                     & (jnp.bitwise_and(ti, size - 1) >= mid) & (jnp.bitwise_and(tj, size - 1) < mid))
            for h, sl in enumerate(heads):
                attn[h] = attn[h] + jnp.where(pairs, _mm_nt(z[:, sl], z[:, sl]), 0.0)
        blocks.append(attn)

    for b in nrows:
        kk, q, vf, v16, sums = pre[b]
        lam = sums[0:c]
        lam_last = lam[c - 1:c, :]
        q_in = (q * jnp.exp(lam)).astype(BF16)
        k_out = (kk * jnp.exp(lam_last - lam)).astype(BF16)
        e_last = jnp.exp(lam_last)
        outs = []
        for h, sl in enumerate(heads):
            s_prev = st[b, h]
            o = (exact[b][h] + _mm(blocks[b][h].astype(BF16), v16[:, sl])
                 + _mm_nt(q_in[:, sl], s_prev.astype(BF16)))
            st[b, h] = s_prev * e_last[:, sl] + _mm_tn(v16[:, sl], k_out[:, sl])
            outs.append(o * lax.rsqrt(jnp.mean(o * o, axis=-1, keepdims=True) + NORM_EPS) * nw_ref[...])
        gate = _silu(p_ref[b, :, 3 * width:4 * width])
        o_ref[b, half * c:(half + 1) * c, :] = (jnp.concatenate(outs, axis=1) * gate).astype(o_ref.dtype)


def _hgrn(x, mod, w, layer, hgrn_lb, norm_w):
    aux = (hgrn_lb.astype(F32), norm_w.reshape(1, HGRN_DK).astype(F32))
    return _fused_mixer("hgrn2", _hgrn_init, functools.partial(_hgrn_chunk, layer), x, mod, w, aux,
                        [(HGRN_HEADS, HGRN_DK, HGRN_DK)])


def _merge_kernel(x_ref, mod_ref, ssd_ref, l0_ref, l1_ref, l2_ref, d0_ref, d1_ref, d2_ref, ret_ref, hgrn_ref,
                  wm_ref, wb_ref, wo_ref, g_ref, b_ref, out_ref):
    m = mod_ref[...]
    x = x_ref[...]
    h = (x * (1.0 + m[1:2]) + m[0:1]).astype(BF16)
    pairs = range(DIL_HEADS // 2)
    dil = _combine_dilated([r[...] for r in (l0_ref, l1_ref, l2_ref)],
                           [[r[p] for p in pairs] for r in (d0_ref, d1_ref, d2_ref)]).astype(BF16)
    y = None
    for i, o in enumerate((ssd_ref[...], dil, ret_ref[...], hgrn_ref[...])):
        gate = jax.nn.sigmoid(_mm(h, wm_ref[:, i * D_MODEL:(i + 1) * D_MODEL]))
        term = gate * _mm(o, wb_ref[i])
        y = term if y is None else y + term
    mix = _mm(y.astype(BF16), wo_ref[...])
    out_ref[...] = _layer_norm_rows(DEEPNORM_ALPHA * x + m[2:3] * mix, g_ref[...], b_ref[...])


def _merge(x2, mod, o_ssd, dil, o_ret, o_hgrn, w_merge, w_branch, w_o, ln_g, ln_b, seq):
    t = x2.shape[0]
    tm = 512
    per_b = seq // tm
    lses, outs = dil
    rows = pl.BlockSpec((tm, D_MODEL), lambda i: (i, 0))
    brow = pl.BlockSpec((tm, BRANCH_WIDTH), lambda i: (i, 0))
    lrow = pl.BlockSpec((tm, LANES), lambda i: (i, 0))
    drow = pl.BlockSpec((DIL_HEADS // 2, tm, LANES), lambda i: (0, i, 0))
    return pl.pallas_call(
        _merge_kernel,
        grid=(t // tm,),
        in_specs=[rows, pl.BlockSpec((None, 6, D_MODEL), lambda i: (i // per_b, 0, 0)), brow]
        + [lrow] * DIL_GROUPS + [drow] * DIL_GROUPS + [brow, brow]
        + [_const_spec(w_merge.shape), _const_spec(w_branch.shape), _const_spec(w_o.shape),
           _const_spec((1, D_MODEL)), _const_spec((1, D_MODEL))],
        out_specs=rows,
        out_shape=jax.ShapeDtypeStruct((t, D_MODEL), F32),
        compiler_params=_cparams("parallel"),
        name="merge_out_ln",
    )(x2, mod, o_ssd, *lses, *outs, o_ret, o_hgrn, w_merge, w_branch, w_o, ln_g.reshape(1, -1), ln_b.reshape(1, -1))


FFN_HALO = SUBLANES


def _ffn_kernel(x_ref, mod_ref, wu_ref, cw_ref, cb_ref, wd_ref, g_ref, b_ref, out_ref, ubuf0, ubuf1, tail, act):
    tm = x_ref.shape[0]
    halo = FFN_HALO

    @pl.when(pl.program_id(1) == 0)
    def _():
        tail[...] = jnp.zeros_like(tail)

    m = mod_ref[...]
    x = x_ref[...]
    h = (x * (1.0 + m[4:5]) + m[3:4]).astype(BF16)

    def cols(ref, j):
        return [ref[:, part * D_FF + j * FF_CHUNK:part * D_FF + (j + 1) * FF_CHUNK] for part in range(2)]

    def up(j, ubuf):
        ubuf[0:halo, :] = tail[j]
        for part, w in enumerate(cols(wu_ref, j)):
            ubuf[halo:halo + tm, part * FF_CHUNK:(part + 1) * FF_CHUNK] = _mm(h, w)
        tail[j] = ubuf[tm:tm + halo, :]

    def gate(j, ubuf):
        cw = jnp.concatenate(cols(cw_ref, j), axis=1)
        u = jnp.broadcast_to(jnp.concatenate(cols(cb_ref, j), axis=1), (tm, 2 * FF_CHUNK))
        for k in range(FFN_CONV):
            off = halo - (FFN_CONV - 1) + k
            u = u + cw[k:k + 1, :] * ubuf[off:off + tm, :]
        act[j] = (_silu(u[:, :FF_CHUNK]) * u[:, FF_CHUNK:]).astype(BF16)

    bufs = (ubuf0, ubuf1)
    up(0, ubuf0)
    for j in range(N_FF_CHUNKS):
        if j + 1 < N_FF_CHUNKS:
            up(j + 1, bufs[(j + 1) % 2])
        gate(j, bufs[j % 2])
    ffn = _mm(act[0], wd_ref[0])
    for j in range(1, N_FF_CHUNKS):
        ffn = ffn + _mm(act[j], wd_ref[j])
    out_ref[...] = _layer_norm_rows(DEEPNORM_ALPHA * x + m[5:6] * ffn, g_ref[...], b_ref[...])


def _ffn(x2, mod, w_up, conv_w, conv_b, w_down, ln_g, ln_b, nb, seq):
    tm = 512
    per_b = seq // tm
    rows = pl.BlockSpec((tm, D_MODEL), lambda b, i: (b * per_b + i, 0))
    return pl.pallas_call(
        _ffn_kernel,
        grid=(nb, per_b),
        in_specs=[rows, pl.BlockSpec((None, 6, D_MODEL), lambda b, i: (b, 0, 0)),
                  _const_spec(w_up.shape), _const_spec(conv_w.shape), _const_spec(conv_b.shape),
                  _const_spec(w_down.shape), _const_spec((1, D_MODEL)), _const_spec((1, D_MODEL))],
        out_specs=rows,
        out_shape=jax.ShapeDtypeStruct((nb * seq, D_MODEL), F32),
        scratch_shapes=[pltpu.VMEM((tm + FFN_HALO, 2 * FF_CHUNK), F32),
                        pltpu.VMEM((tm + FFN_HALO, 2 * FF_CHUNK), F32),
                        pltpu.VMEM((N_FF_CHUNKS, FFN_HALO, 2 * FF_CHUNK), F32),
                        pltpu.VMEM((N_FF_CHUNKS, tm, FF_CHUNK), BF16)],
        compiler_params=_cparams("parallel", "arbitrary"),
        name="conv_ffn_ln",
    )(x2, mod, w_up, conv_w, conv_b, w_down, ln_g.reshape(1, -1), ln_b.reshape(1, -1))


def _pack_in_proj(w_in):
    offs = np.concatenate([[0], np.cumsum(SECTION_SIZES)])
    sec = {n: w_in[:, offs[i]:offs[i + 1]] for i, n in enumerate(SECTION_NAMES)}
    w = BRANCH_WIDTH
    sec.update({f"q{g}": sec["dq"][:, g * w:(g + 1) * w] for g in range(DIL_GROUPS)}, k=sec["dk"], v=sec["dv"])
    sec["dt_pad"] = jnp.zeros((D_MODEL, LANES - SSD_HEADS), w_in.dtype)
    cat = lambda names: jnp.concatenate([sec[n] for n in names], axis=1).astype(BF16)
    return {"dil": cat(sorted(PD_COLS, key=PD_COLS.get)), "ssd": cat(("z", "xbc", "dt", "dt_pad")),
            "ret": cat(("rq", "rk", "rv", "rg")), "hgrn": cat(("hq", "hf", "hi", "hg"))}


def kernel(x, c, w_ada, b_ada, w_in, w_merge, ssd_conv_w, ssd_conv_b, ssd_dt_bias, ssd_a_log, ssd_d, ssd_norm_w, hgrn_lb, hgrn_norm_w, w_branch_out, w_o, ln1_g, ln1_b, w_up, ffn_conv_w, ffn_conv_b, w_down, ln2_g, ln2_b):
    nb, seq, _ = x.shape
    assert seq % (DIL_PATTERNS[-1][1] * DIL_QBLOCKS * DIL_BLOCK) == 0 and x.shape[-1] == D_MODEL
    mods = _modulation(c, w_ada, b_ada).reshape(DEPTH, nb, 6, D_MODEL)
    x2 = x.reshape(nb * seq, D_MODEL)
    for layer in range(DEPTH):
        mod = mods[layer]
        w_proj = _pack_in_proj(w_in[layer])
        x3 = x2.reshape(nb, seq, D_MODEL)
        o_dil = _dilated([a.reshape(nb, seq, -1) for a in _project_dilated(x2, mod, w_proj["dil"], seq)])
        o_ssd = _ssd(x3, mod, w_proj["ssd"], ssd_conv_w[layer], ssd_conv_b[layer], ssd_dt_bias[layer],
                     ssd_a_log[layer], ssd_d[layer], ssd_norm_w[layer])
        o_ret = _retention(x3, mod, w_proj["ret"])
        o_hgrn = _hgrn(x3, mod, w_proj["hgrn"], layer, hgrn_lb, hgrn_norm_w[layer])
        flat = lambda a: a.reshape(nb * seq, BRANCH_WIDTH)
        x2 = _merge(x2, mod, flat(o_ssd), o_dil, flat(o_ret), flat(o_hgrn), w_merge[layer].astype(BF16),
                    w_branch_out[layer].astype(BF16), w_o[layer].astype(BF16), ln1_g[layer], ln1_b[layer], seq)
        x2 = _ffn(x2, mod, w_up[layer].astype(BF16), ffn_conv_w[layer], ffn_conv_b[layer].reshape(1, -1),
                  w_down[layer].reshape(N_FF_CHUNKS, FF_CHUNK, D_MODEL).astype(BF16),
                  ln2_g[layer], ln2_b[layer], nb, seq)
    return x2.reshape(nb, seq, D_MODEL)
```

```python
import functools
import math

import numpy as np
import jax
import jax.numpy as jnp
from jax import lax
from jax.experimental import pallas as pl
from jax.experimental.pallas import tpu as pltpu

F32 = jnp.float32
BF16 = jnp.bfloat16

D_MODEL = 1024
DEPTH = 2
DEEPNORM_ALPHA = (2 * DEPTH) ** 0.25
NORM_EPS = 1e-5

BRANCH_WIDTH = D_MODEL // 2

SSD_HEAD_DIM = 64
SSD_HEADS = BRANCH_WIDTH // SSD_HEAD_DIM
SSD_GROUPS = 2
SSD_STATE = 64
SSD_CONV = 4
SSD_CONV_DIM = BRANCH_WIDTH + 2 * SSD_GROUPS * SSD_STATE

DIL_HEAD_DIM = 64
DIL_HEADS = BRANCH_WIDTH // DIL_HEAD_DIM
DIL_PATTERNS = ((128, 1), (512, 4), (2048, 16))
DIL_GROUPS = len(DIL_PATTERNS)
DIL_BLOCK = 128

RET_HEADS = 4
RET_DK = BRANCH_WIDTH // RET_HEADS

HGRN_HEADS = 4
HGRN_DK = BRANCH_WIDTH // HGRN_HEADS
HGRN_SUB = 4

D_FF = 256 * ((8 * D_MODEL // 3 + 255) // 256)
FFN_CONV = 3
FF_CHUNK = 256
N_FF_CHUNKS = D_FF // FF_CHUNK

SECTION_SIZES = (
    BRANCH_WIDTH, SSD_CONV_DIM, SSD_HEADS,
    DIL_GROUPS * BRANCH_WIDTH, BRANCH_WIDTH, BRANCH_WIDTH,
    BRANCH_WIDTH, BRANCH_WIDTH, BRANCH_WIDTH, BRANCH_WIDTH,
    BRANCH_WIDTH, BRANCH_WIDTH, BRANCH_WIDTH, BRANCH_WIDTH,
)
SECTION_NAMES = ("z", "xbc", "dt", "dq", "dk", "dv", "rq", "rk", "rv", "rg", "hq", "hf", "hi", "hg")

LANES = 128
SUBLANES = 8
V7X_VMEM_BYTES = 64 * 1024 * 1024
VMEM_LIMIT = V7X_VMEM_BYTES - 8 * 1024 * 1024
CHUNK = 128
PD_COLS = {n: i for i, n in enumerate(("q0", "q1", "k", "v", "q2"))}
MIXER_ROWS = 4


def _log2(n):
    return int(math.log2(n))


def _rows_per_step(nb):
    return MIXER_ROWS if nb % MIXER_ROWS == 0 else 1


def _cparams(*sem):
    return pltpu.CompilerParams(dimension_semantics=sem, vmem_limit_bytes=VMEM_LIMIT)


def _silu(v):
    return v * jax.nn.sigmoid(v)


def _mm(a, b):
    return jnp.dot(a, b, preferred_element_type=F32)


def _mm_nt(a, b):
    return lax.dot_general(a, b, (((1,), (1,)), ((), ())), preferred_element_type=F32)


def _mm_tn(a, b):
    return lax.dot_general(a, b, (((0,), (0,)), ((), ())), preferred_element_type=F32)


def _split2(a):
    hi = a.astype(BF16)
    lo = (a - hi.astype(F32)).astype(BF16)
    return hi, lo


def _split3(a):
    hi = a.astype(BF16)
    r = a - hi.astype(F32)
    mid = r.astype(BF16)
    lo = (r - mid.astype(F32)).astype(BF16)
    return hi, mid, lo


def _mm_f32(a, b):
    ah, al = _split2(a)
    bh, bl = _split2(b)
    return _mm(ah, bh) + (_mm(ah, bl) + _mm(al, bh))


def _cumsum_rows(tri, a, lo_rows=None):
    hi, mid, lo = _split3(a)
    main = _mm(jnp.concatenate([tri, tri], axis=1), jnp.concatenate([hi, mid], axis=0))
    if lo_rows is None:
        return main + _mm(tri, lo)
    return jnp.concatenate([main[:lo_rows] + _mm(tri[:lo_rows], lo), main[lo_rows:]], axis=0)


def _expand(a, e):
    hi, lo = _split2(a)
    return _mm(jnp.concatenate([hi, lo], axis=1), jnp.concatenate([e, e], axis=0))


def _block_diag(a):
    d = a.shape[1] // 2
    zero = jnp.zeros((a.shape[0], d), a.dtype)
    return jnp.concatenate([jnp.concatenate([a[:, :d], zero], axis=1),
                            jnp.concatenate([zero, a[:, d:]], axis=1)], axis=0)


def _iota(shape, dim):
    return lax.broadcasted_iota(jnp.int32, shape, dim)


def _tri(n):
    return jnp.where(_iota((n, n), 1) <= _iota((n, n), 0), 1.0, 0.0).astype(BF16)


def _head_expander(width):
    shape = (LANES, BRANCH_WIDTH)
    shift = _log2(width)
    return jnp.where(jnp.right_shift(_iota(shape, 1), shift) == _iota(shape, 0), 1.0, 0.0).astype(BF16)


def _layer_norm_rows(v, g, b):
    vc = v - jnp.mean(v, axis=-1, keepdims=True)
    return vc * lax.rsqrt(jnp.mean(vc * vc, axis=-1, keepdims=True) + NORM_EPS) * g + b


def _mod_kernel(c_ref, w_ref, b_ref, o_ref):
    o_ref[0] = _mm_f32(_silu(c_ref[...]), w_ref[0]) + b_ref[0]


def _modulation(c, w_ada, b_ada):
    nb = c.shape[0]
    tn = 1536
    return pl.pallas_call(
        _mod_kernel,
        grid=(DEPTH, 6 * D_MODEL // tn),
        in_specs=[pl.BlockSpec((nb, D_MODEL), lambda l, j: (0, 0)),
                  pl.BlockSpec((1, D_MODEL, tn), lambda l, j: (l, 0, j)),
                  pl.BlockSpec((1, 1, tn), lambda l, j: (l, 0, j))],
        out_specs=pl.BlockSpec((1, nb, tn), lambda l, j: (l, 0, j)),
        out_shape=jax.ShapeDtypeStruct((DEPTH, nb, 6 * D_MODEL), F32),
        compiler_params=_cparams("parallel", "parallel"),
        name="adaln_mod",
    )(c, w_ada, b_ada.reshape(DEPTH, 1, 6 * D_MODEL))


def _const_spec(shape):
    zeros = (0,) * len(shape)
    return pl.BlockSpec(shape, lambda *_: zeros, pipeline_mode=pl.Buffered(1))


def _destride_matrix(n, r):
    per = n // r
    i = _iota((n, n), 0)
    src = jnp.left_shift(jnp.bitwise_and(i, per - 1), _log2(r)) + jnp.right_shift(i, _log2(per))
    return jnp.where(_iota((n, n), 1) == src, 1.0, 0.0).astype(BF16)


def _proj_kernel(x_ref, mod_ref, wd_ref, od0_ref, od1_ref, od2_ref):
    m = mod_ref[...]
    h = (x_ref[...] * (1.0 + m[1:2]) + m[0:1]).astype(BF16)
    d = _mm(h, wd_ref[...]).astype(BF16)
    w = BRANCH_WIDTH
    tm = d.shape[0]
    od0_ref[...] = jnp.concatenate([d[:, 0:w], d[:, 2 * w:4 * w]], axis=1)
    od1_ref[...] = _mm(_destride_matrix(tm, DIL_PATTERNS[1][1]), d[:, w:4 * w]).astype(BF16)
    od2_ref[...] = _mm(_destride_matrix(tm, DIL_PATTERNS[2][1]), d[:, 2 * w:5 * w]).astype(BF16)


PROJ_ROWS = 256
DIL_OPERAND_COLS = ((0, 1, 2), (0, 1, 2), (2, 0, 1))


def _project_dilated(x2, mod, wd, seq):
    t = x2.shape[0]
    tm = PROJ_ROWS
    per_b = seq // tm
    qkv = 3 * BRANCH_WIDTH
    return pl.pallas_call(
        _proj_kernel,
        grid=(t // tm,),
        in_specs=[pl.BlockSpec((tm, D_MODEL), lambda i: (i, 0)),
                  pl.BlockSpec((None, 6, D_MODEL), lambda i: (i // per_b, 0, 0)),
                  _const_spec(wd.shape)],
        out_specs=[pl.BlockSpec((tm, qkv), lambda i: (i, 0))] * DIL_GROUPS,
        out_shape=[jax.ShapeDtypeStruct((t, qkv), BF16)] * DIL_GROUPS,
        compiler_params=_cparams("parallel"),
        name="in_proj_dilated",
    )(x2, mod, wd)


SSD_HALO = SUBLANES


def _ssd_init(xbuf, st):
    xbuf[:, 0:SSD_HALO, :] = jnp.zeros((xbuf.shape[0], SSD_HALO, SSD_CONV_DIM), F32)
    st[...] = jnp.zeros_like(st)


def _ssd_chunk(p_ref, o_ref, half, cw_ref, cb_ref, dtb_ref, alog_ref, dsk_ref, nw_ref, xbuf, st):
    c = CHUNK
    halo = SSD_HALO
    xbc0 = BRANCH_WIDTH
    dt0 = BRANCH_WIDTH + SSD_CONV_DIM
    for b in range(p_ref.shape[0]):
        xbuf[b, halo:halo + c, :] = p_ref[b, :, xbc0:dt0]
        acc = jnp.broadcast_to(cb_ref[...], (c, SSD_CONV_DIM))
        for k in range(SSD_CONV):
            off = halo - (SSD_CONV - 1) + k
            acc = acc + cw_ref[k:k + 1, :] * xbuf[b, off:off + c, :]
        xbuf[b, 0:halo, :] = xbuf[b, c:c + halo, :]
        y = _silu(acc)
        xs = y[:, :BRANCH_WIDTH]
        bm = y[:, BRANCH_WIDTH:BRANCH_WIDTH + LANES]
        cm = y[:, BRANCH_WIDTH + LANES:]

        dtr = p_ref[b, :, dt0:dt0 + LANES] + dtb_ref[...]
        dt = jnp.maximum(dtr, 0.0) + jnp.log1p(jnp.exp(-jnp.abs(dtr)))
        da = dt * (-jnp.exp(alog_ref[...]))
        cs = _cumsum_rows(_tri(c), da)
        cs_t = cs.T
        tot = cs[c - 1:c, :]

        stack = jnp.concatenate([dt, jnp.exp(tot - cs), jnp.exp(cs), jnp.broadcast_to(jnp.exp(tot), (8, LANES))], axis=0)
        ex = _expand(stack, _head_expander(SSD_HEAD_DIM))
        dt_e, ds_e, ecs_e, dec_e = ex[0:c], ex[c:2 * c], ex[2 * c:3 * c], ex[3 * c:3 * c + 1]

        xdt = xs * dt_e
        xds = xdt * ds_e
        causal = _iota((c, c), 1) <= _iota((c, c), 0)
        lane = _iota((c, LANES), 1)
        bm16 = bm.astype(BF16)
        cbs = []
        for g in range(SSD_GROUPS):
            cm_g = jnp.where(jnp.right_shift(lane, _log2(SSD_STATE)) == g, cm, 0.0).astype(BF16)
            cbs.append(_mm_nt(cm_g, bm16))
        parts = []
        for p in range(SSD_HEADS // 2):
            g = (2 * p) // (SSD_HEADS // SSD_GROUPS)
            ms = []
            for e in range(2):
                h = 2 * p + e
                diff = cs[:, h:h + 1] - cs_t[h:h + 1, :]
                ms.append((cbs[g] * jnp.exp(jnp.where(causal, diff, -jnp.inf))).astype(BF16))
            xp = xdt[:, p * LANES:(p + 1) * LANES]
            xbd = jnp.concatenate([jnp.where(lane < SSD_HEAD_DIM, xp, 0.0), jnp.where(lane >= SSD_HEAD_DIM, xp, 0.0)],
                                  axis=0).astype(BF16)
            parts.append(_mm(jnp.concatenate(ms, axis=1), xbd))
        y_diag = jnp.concatenate(parts, axis=1)

        s_prev = st[b]
        y_off = _mm(cm.astype(BF16), s_prev.astype(BF16)) * ecs_e
        upd = _mm(bm.T.astype(BF16), xds.astype(BF16))
        shape = (LANES, BRANCH_WIDTH)
        same_group = (jnp.right_shift(_iota(shape, 0), _log2(SSD_STATE))
                      == jnp.right_shift(_iota(shape, 1), _log2(BRANCH_WIDTH // SSD_GROUPS)))
        st[b] = s_prev * dec_e + jnp.where(same_group, upd, 0.0)

        yv = (y_diag + y_off + xs * dsk_ref[...]) * _silu(p_ref[b, :, 0:BRANCH_WIDTH])
        gw = BRANCH_WIDTH // SSD_GROUPS
        outs = []
        for g in range(SSD_GROUPS):
            yg = yv[:, g * gw:(g + 1) * gw]
            outs.append(yg * lax.rsqrt(jnp.mean(yg * yg, axis=-1, keepdims=True) + NORM_EPS))
        o_ref[b, half * c:(half + 1) * c, :] = (jnp.concatenate(outs, axis=1) * nw_ref[...]).astype(o_ref.dtype)


def _ssd(x, mod, w, conv_w, conv_b, dt_bias, a_log, d_skip, norm_w):
    pad = LANES - SSD_HEADS
    row = lambda v: v.reshape(1, -1).astype(F32)
    aux = (conv_w.astype(F32), row(conv_b), row(jnp.pad(dt_bias, (0, pad))), row(jnp.pad(a_log, (0, pad))),
           row(jnp.repeat(d_skip, SSD_HEAD_DIM)), row(norm_w))
    return _fused_mixer("ssd_mixer", _ssd_init, _ssd_chunk, x, mod, w, aux,
                        [(CHUNK + SSD_HALO, SSD_CONV_DIM), (LANES, BRANCH_WIDTH)])


def _alibi_slopes(n):
    def pow2(k):
        start = 2.0 ** (-8.0 / k)
        return [start ** (i + 1) for i in range(k)]
    if math.log2(n).is_integer():
        s = pow2(n)
    else:
        c = 2 ** math.floor(math.log2(n))
        s = pow2(c) + pow2(2 * c)[0::2][: n - c]
    return [float(np.float32(v)) for v in s]


def _dil_kernel(dilation, n_back, slopes, q_ref, kc_ref, kp_ref, vc_ref, vp_ref, o_ref, lse_ref):
    blk = DIL_BLOCK
    rho = pl.program_id(2)
    log2e = 1.0 / math.log(2.0)

    def rows_of(ref, start, sl):
        if len(ref.shape) == 2:
            return ref[start:start + blk, sl]
        per = ref.shape[1]
        return jnp.concatenate([ref[t, :, sl] for t in range(start // per, (start + blk) // per)], axis=0)

    qi = _iota((blk, 2 * blk), 0)
    kj = _iota((blk, 2 * blk), 1)
    dist = qi - kj + blk
    in_window = (dist >= 0) & (dist <= n_back)
    neg_dist2 = (dist * dilation).astype(F32) * (-log2e)
    lane = _iota((blk, LANES), 1)
    for sb in range(DIL_QBLOCKS):
        if sb == 0:
            valid = in_window & (kj >= jnp.where(pl.program_id(1) > 0, 0, blk))
        else:
            valid = in_window
        bias = jnp.where(valid, neg_dist2, -jnp.inf)
        rows = pl.ds(rho + sb * blk * dilation, blk, stride=dilation) if dilation > 1 else pl.ds(sb * blk, blk)
        lse_tile = jnp.zeros((blk, LANES), F32)
        for p in range(DIL_HEADS // 2):
            sl = slice(p * LANES, (p + 1) * LANES)
            qp = rows_of(q_ref, sb * blk, sl).astype(F32) * (DIL_HEAD_DIM ** -0.5 * log2e)
            if sb == 0:
                kk = jnp.concatenate([rows_of(kp_ref, 0, sl), rows_of(kc_ref, 0, sl)], axis=0)
                vv = jnp.concatenate([rows_of(vp_ref, 0, sl), rows_of(vc_ref, 0, sl)], axis=0)
            else:
                kk = jnp.concatenate([rows_of(kc_ref, (sb - 1) * blk, sl), rows_of(kc_ref, sb * blk, sl)], axis=0)
                vv = jnp.concatenate([rows_of(vc_ref, (sb - 1) * blk, sl), rows_of(vc_ref, sb * blk, sl)], axis=0)
            pair = []
            for e in range(2):
                h = 2 * p + e
                own = (lane < DIL_HEAD_DIM) if e == 0 else (lane >= DIL_HEAD_DIM)
                qm = jnp.where(own, qp, 0.0).astype(BF16)
                s = _mm_nt(qm, kk) + slopes[h] * bias
                m = jnp.max(s, axis=-1, keepdims=True)
                pexp = jnp.exp2(s - m)
                l = jnp.sum(pexp, axis=-1, keepdims=True)
                pair.append(_mm(pexp.astype(BF16), vv) / l)
                lse_tile = jnp.where(lane == h, (m + jnp.log2(l)) * math.log(2.0), lse_tile)
            o_ref[p, rows, :] = jnp.where(lane < DIL_HEAD_DIM, pair[0], pair[1])
        lse_ref[rows, :] = lse_tile


DIL_QBLOCKS = 2


def _dilated_group(pd, g):
    nb, seq, _ = pd.shape
    window, r = DIL_PATTERNS[g]
    n = seq // r
    qrows = DIL_QBLOCKS * DIL_BLOCK
    w = BRANCH_WIDTH
    qc, kc, vc = DIL_OPERAND_COLS[g]
    if r == 1:
        view = pd
        cur = lambda col: pl.BlockSpec((None, qrows, w), lambda b, i, rho: (b, i, col))
        prev = lambda col: pl.BlockSpec((None, DIL_BLOCK, w),
                                        lambda b, i, rho: (b, jnp.maximum(DIL_QBLOCKS * i - 1, 0), col))
    else:
        per = PROJ_ROWS // r
        view = pd.reshape(nb, seq // PROJ_ROWS, r, per, 3 * w)
        cur = lambda col: pl.BlockSpec((None, qrows // per, None, per, w), lambda b, i, rho: (b, i, rho, 0, col))
        prev = lambda col: pl.BlockSpec((None, DIL_BLOCK // per, None, per, w),
                                        lambda b, i, rho: (b, jnp.maximum(DIL_QBLOCKS * i - 1, 0), rho, 0, col))
    slopes = _alibi_slopes(DIL_GROUPS * DIL_HEADS)[g * DIL_HEADS:(g + 1) * DIL_HEADS]
    pairs = DIL_HEADS // 2
    return pl.pallas_call(
        functools.partial(_dil_kernel, r, window // r, slopes),
        grid=(nb, n // qrows, r),
        in_specs=[cur(qc), cur(kc), prev(kc), cur(vc), prev(vc)],
        out_specs=[pl.BlockSpec((pairs, None, qrows * r, LANES), lambda b, i, rho: (0, b, i, 0)),
                   pl.BlockSpec((None, qrows * r, LANES), lambda b, i, rho: (b, i, 0))],
        out_shape=[jax.ShapeDtypeStruct((pairs, nb, seq, LANES), F32),
                   jax.ShapeDtypeStruct((nb, seq, LANES), F32)],
        compiler_params=_cparams("parallel", "parallel", "arbitrary"),
        name=f"dilated_attn_g{g}",
    )(view, view, view, view, view)


def _combine_dilated(lses, outs):
    m = functools.reduce(jnp.maximum, lses)
    es = [jnp.exp(l - m) for l in lses]
    den = functools.reduce(lambda a, b: a + b, es)
    e = _head_expander(DIL_HEAD_DIM)
    acc = None
    for ev, o in zip(es, outs):
        term = _expand(ev / den, e) * jnp.concatenate(o, axis=1)
        acc = term if acc is None else acc + term
    return acc


def _dilated(pds):
    nb, seq, _ = pds[0].shape
    res = [_dilated_group(pd, g) for g, pd in enumerate(pds)]
    t = nb * seq
    return [r[1].reshape(t, LANES) for r in res], [r[0].reshape(DIL_HEADS // 2, t, LANES) for r in res]


def _fused_mixer_kernel(init_fn, chunk_fn, n_aux, x0_ref, xa_ref, xb_ref, mod_ref, modn_ref, w_ref, *rest):
    aux, o_ref, p0, p1, scratch = rest[:n_aux], rest[n_aux], rest[n_aux + 1], rest[n_aux + 2], rest[n_aux + 3:]
    rows = x0_ref.shape[0]
    m = mod_ref[...]
    wrap = pl.program_id(1) == pl.num_programs(1) - 1

    def project(x_ref, p_ref, mv):
        h = jnp.concatenate([(x_ref[b] * (1.0 + mv[b, 1:2]) + mv[b, 0:1]).astype(BF16) for b in range(rows)], axis=0)
        res = _mm(h, w_ref[...])
        for b in range(rows):
            p_ref[b] = res[b * CHUNK:(b + 1) * CHUNK]

    @pl.when(pl.program_id(1) == 0)
    def _():
        init_fn(*scratch)

    @pl.when((pl.program_id(0) == 0) & (pl.program_id(1) == 0))
    def _():
        project(x0_ref, p0, m)

    project(xa_ref, p1, m)
    chunk_fn(p0, o_ref, 0, *aux, *scratch)
    project(xb_ref, p0, jnp.where(wrap, modn_ref[...], m))
    chunk_fn(p1, o_ref, 1, *aux, *scratch)


def _fused_mixer(name, init_fn, chunk_fn, x, mod, w, aux, scratch_shapes):
    nb, seq, _ = x.shape
    rows = _rows_per_step(nb)
    nc = seq // CHUNK
    n = w.shape[1]
    steps = nc // 2
    last_b = nb // rows - 1
    xblock = (rows, CHUNK, D_MODEL)

    def next_chunk(b, s):
        wrap = s == steps - 1
        return jnp.where(wrap, jnp.minimum(b + 1, last_b), b), jnp.where(wrap, 0, 2 * s + 2), 0

    return pl.pallas_call(
        functools.partial(_fused_mixer_kernel, init_fn, chunk_fn, len(aux)),
        grid=(nb // rows, steps),
        in_specs=[pl.BlockSpec(xblock, lambda b, s: (0, 0, 0)), pl.BlockSpec(xblock, lambda b, s: (b, 2 * s + 1, 0)),
                  pl.BlockSpec(xblock, next_chunk),
                  pl.BlockSpec((rows, 6, D_MODEL), lambda b, s: (b, 0, 0)),
                  pl.BlockSpec((rows, 6, D_MODEL), lambda b, s: (jnp.minimum(b + 1, last_b), 0, 0)),
                  _const_spec(w.shape)]
        + [_const_spec(a.shape) for a in aux],
        out_specs=pl.BlockSpec((rows, 2 * CHUNK, BRANCH_WIDTH), lambda b, s: (b, s, 0)),
        out_shape=jax.ShapeDtypeStruct((nb, seq, BRANCH_WIDTH), BF16),
        scratch_shapes=[pltpu.VMEM((rows, CHUNK, n), F32), pltpu.VMEM((rows, CHUNK, n), F32)]
        + [pltpu.VMEM((rows,) + s, F32) for s in scratch_shapes],
        compiler_params=_cparams("arbitrary", "arbitrary"),
        name=name,
    )(x, x, x, mod, mod, w, *aux)


def _ret_init(st):
    st[...] = jnp.zeros_like(st)


def _ret_chunk(p_ref, o_ref, half, st):
    c = CHUNK
    w = BRANCH_WIDTH
    dk = RET_DK
    scale = dk ** -0.5
    row = _iota((c, c), 0)
    col = _iota((c, c), 1)
    rel = (row - col).astype(F32)
    pos = row.astype(F32)
    lgs = [math.log(1.0 - 2.0 ** (-5.0 - h)) for h in range(RET_HEADS)]

    def pair_of(f):
        return [jnp.concatenate([f(lgs[2 * pr]), f(lgs[2 * pr + 1])], axis=1) for pr in range(RET_HEADS // 2)]

    decay = pair_of(lambda lg: jnp.where(row >= col, jnp.exp(lg * jnp.maximum(rel, 0.0)), 0.0) * scale)
    q_scale = pair_of(lambda lg: jnp.exp(lg * (pos + 1.0)))
    k_scale = pair_of(lambda lg: jnp.exp(lg * (c - 1.0 - pos)) * scale)
    on_diag = (jnp.right_shift(_iota((2 * dk, 2 * dk), 0), _log2(dk))
               == jnp.right_shift(_iota((2 * dk, 2 * dk), 1), _log2(dk)))
    upper = _iota((2 * dk, 2 * dk), 0) < dk
    for b in range(p_ref.shape[0]):
        outs = []
        for pr in range(RET_HEADS // 2):
            sl = lambda i: slice(i * w + pr * 2 * dk, i * w + (pr + 1) * 2 * dk)
            qf, kf = p_ref[b, :, sl(0)], p_ref[b, :, sl(1)]
            v16 = p_ref[b, :, sl(2)].astype(BF16)
            scores = _mm_nt(qf.astype(BF16), _block_diag(kf.astype(BF16))) * decay[pr]
            inner = _mm(scores.astype(BF16), _block_diag(v16))
            s_prev = st[b, pr]
            o = inner + _mm((qf * q_scale[pr]).astype(BF16), s_prev.astype(BF16))
            upd = _mm_tn((kf * k_scale[pr]).astype(BF16), v16)
            carry = jnp.where(upper, math.exp(lgs[2 * pr] * c), math.exp(lgs[2 * pr + 1] * c))
            st[b, pr] = s_prev * carry + jnp.where(on_diag, upd, 0.0)
            for e in range(2):
                oh = o[:, e * dk:(e + 1) * dk]
                oc = oh - jnp.mean(oh, axis=-1, keepdims=True)
                outs.append(oc * lax.rsqrt(jnp.mean(oc * oc, axis=-1, keepdims=True) + NORM_EPS))
        gate = _silu(p_ref[b, :, 3 * w:4 * w])
        o_ref[b, half * c:(half + 1) * c, :] = (jnp.concatenate(outs, axis=1) * gate).astype(o_ref.dtype)


def _retention(x, mod, w):
    return _fused_mixer("retention", _ret_init, _ret_chunk, x, mod, w, (),
                        [(RET_HEADS // 2, 2 * RET_DK, 2 * RET_DK)])


def _hgrn_init(st):
    st[...] = jnp.zeros_like(st)


def _hgrn_chunk(layer, p_ref, o_ref, half, lb_ref, nw_ref, st):
    c = CHUNK
    width = BRANCH_WIDTH
    neg_inf = -jnp.inf

    rows = [lb_ref[l:l + 1, :] for l in range(DEPTH)]
    mx = functools.reduce(jnp.maximum, rows)
    es = [jnp.exp(r - mx) for r in rows]
    den = functools.reduce(lambda a, b: a + b, es)
    sm = [e / den for e in es]
    lb = functools.reduce(lambda a, b: a + b, sm[:layer + 1]) - sm[0]

    ti = _iota((c, c), 0)
    tj = _iota((c, c), 1)
    le_t = jnp.where(tj <= ti, 1.0, 0.0)
    sizes = [c >> k for k in range(_log2(c // HGRN_SUB))]
    mats = [le_t]
    for size in sizes:
        sh = _log2(size)
        last_lower = jnp.left_shift(jnp.right_shift(ti, sh), sh) + (size // 2 - 1)
        mats.append(le_t - jnp.where(tj <= last_lower, 1.0, 0.0))
    seg = jnp.concatenate(mats, axis=0).astype(BF16)
    rowi = _iota((c, width), 0)
    sub = jnp.bitwise_and(rowi, HGRN_SUB - 1)
    heads = [slice(h * HGRN_DK, (h + 1) * HGRN_DK) for h in range(HGRN_HEADS)]

    nrows = range(p_ref.shape[0])
    pre = []
    for b in nrows:
        forget = lb + (1.0 - lb) * jax.nn.sigmoid(p_ref[b, :, width:2 * width])
        lf = jnp.log(forget)
        kk = 1.0 - forget
        q = _silu(p_ref[b, :, 0:width])
        vf = p_ref[b, :, 2 * width:3 * width]
        pre.append((kk, q, vf, vf.astype(BF16), _cumsum_rows(seg, lf, lo_rows=c)))

    exact = []
    for kk, q, vf, v16, sums in pre:
        lam = sums[0:c]
        o_acc = [jnp.zeros((c, HGRN_DK), F32) for _ in heads]
        for d in range(HGRN_SUB):
            if d == 0:
                prod = q * kk
                vd = vf
            else:
                ld = pltpu.roll(lam, d, 0)
                prod = q * pltpu.roll(kk, d, 0) * jnp.exp(jnp.where(sub >= d, lam - ld, neg_inf))
                vd = pltpu.roll(vf, d, 0)
            for h, sl in enumerate(heads):
                o_acc[h] = o_acc[h] + jnp.sum(prod[:, sl], axis=-1, keepdims=True) * vd[:, sl]
        exact.append(o_acc)

    blocks = []
    for kk, q, vf, v16, sums in pre:
        attn = [jnp.zeros((c, c), F32) for _ in heads]
        for k, size in enumerate(sizes):
            mid = size // 2
            rel = sums[(k + 1) * c:(k + 2) * c]
            upper = jnp.bitwise_and(rowi, size - 1) >= mid
            z = (jnp.where(upper, q, kk) * jnp.exp(-jnp.abs(rel))).astype(BF16)
            sh = _log2(size)
            pairs = ((jnp.right_shift(ti, sh) == jnp.right_shift(tj, sh))
                     & (jnp.bitwise_and(ti, size - 1) >= mid) & (jnp.bitwise_and(tj, size - 1) < mid))
            for h, sl in enumerate(heads):
                attn[h] = attn[h] + jnp.where(pairs, _mm_nt(z[:, sl], z[:, sl]), 0.0)
        blocks.append(attn)

    for b in nrows:
        kk, q, vf, v16, sums = pre[b]
        lam = sums[0:c]
        lam_last = lam[c - 1:c, :]
        q_in = (q * jnp.exp(lam)).astype(BF16)
        k_out = (kk * jnp.exp(lam_last - lam)).astype(BF16)
        e_last = jnp.exp(lam_last)
        outs = []
        for h, sl in enumerate(heads):
            s_prev = st[b, h]
            o = (exact[b][h] + _mm(blocks[b][h].astype(BF16), v16[:, sl])
                 + _mm_nt(q_in[:, sl], s_prev.astype(BF16)))
            st[b, h] = s_prev * e_last[:, sl] + _mm_tn(v16[:, sl], k_out[:, sl])
            outs.append(o * lax.rsqrt(jnp.mean(o * o, axis=-1, keepdims=True) + NORM_EPS) * nw_ref[...])
        gate = _silu(p_ref[b, :, 3 * width:4 * width])
        o_ref[b, half * c:(half + 1) * c, :] = (jnp.concatenate(outs, axis=1) * gate).astype(o_ref.dtype)


def _hgrn(x, mod, w, layer, hgrn_lb, norm_w):
    aux = (hgrn_lb.astype(F32), norm_w.reshape(1, HGRN_DK).astype(F32))
    return _fused_mixer("hgrn2", _hgrn_init, functools.partial(_hgrn_chunk, layer), x, mod, w, aux,
                        [(HGRN_HEADS, HGRN_DK, HGRN_DK)])


def _merge_kernel(x_ref, mod_ref, ssd_ref, l0_ref, l1_ref, l2_ref, d0_ref, d1_ref, d2_ref, ret_ref, hgrn_ref,
                  wm_ref, wb_ref, wo_ref, g_ref, b_ref, out_ref):
    m = mod_ref[...]
    x = x_ref[...]
    h = (x * (1.0 + m[1:2]) + m[0:1]).astype(BF16)
    pairs = range(DIL_HEADS // 2)
    dil = _combine_dilated([r[...] for r in (l0_ref, l1_ref, l2_ref)],
                           [[r[p] for p in pairs] for r in (d0_ref, d1_ref, d2_ref)]).astype(BF16)
    y = None
    for i, o in enumerate((ssd_ref[...], dil, ret_ref[...], hgrn_ref[...])):
        gate = jax.nn.sigmoid(_mm(h, wm_ref[:, i * D_MODEL:(i + 1) * D_MODEL]))
        term = gate * _mm(o, wb_ref[i])
        y = term if y is None else y + term
    mix = _mm(y.astype(BF16), wo_ref[...])
    out_ref[...] = _layer_norm_rows(DEEPNORM_ALPHA * x + m[2:3] * mix, g_ref[...], b_ref[...])


def _merge(x2, mod, o_ssd, dil, o_ret, o_hgrn, w_merge, w_branch, w_o, ln_g, ln_b, seq):
    t = x2.shape[0]
    tm = 512
    per_b = seq // tm
    lses, outs = dil
    rows = pl.BlockSpec((tm, D_MODEL), lambda i: (i, 0))
    brow = pl.BlockSpec((tm, BRANCH_WIDTH), lambda i: (i, 0))
    lrow = pl.BlockSpec((tm, LANES), lambda i: (i, 0))
    drow = pl.BlockSpec((DIL_HEADS // 2, tm, LANES), lambda i: (0, i, 0))
    return pl.pallas_call(
        _merge_kernel,
        grid=(t // tm,),
        in_specs=[rows, pl.BlockSpec((None, 6, D_MODEL), lambda i: (i // per_b, 0, 0)), brow]
        + [lrow] * DIL_GROUPS + [drow] * DIL_GROUPS + [brow, brow]
        + [_const_spec(w_merge.shape), _const_spec(w_branch.shape), _const_spec(w_o.shape),
           _const_spec((1, D_MODEL)), _const_spec((1, D_MODEL))],
        out_specs=rows,
        out_shape=jax.ShapeDtypeStruct((t, D_MODEL), F32),
        compiler_params=_cparams("parallel"),
        name="merge_out_ln",
    )(x2, mod, o_ssd, *lses, *outs, o_ret, o_hgrn, w_merge, w_branch, w_o, ln_g.reshape(1, -1), ln_b.reshape(1, -1))


FFN_HALO = SUBLANES


def _ffn_kernel(x_ref, mod_ref, wu_ref, cw_ref, cb_ref, wd_ref, g_ref, b_ref, out_ref, ubuf0, ubuf1, tail, act):
    tm = x_ref.shape[0]
    halo = FFN_HALO

    @pl.when(pl.program_id(1) == 0)
    def _():
        tail[...] = jnp.zeros_like(tail)

    m = mod_ref[...]
    x = x_ref[...]
    h = (x * (1.0 + m[4:5]) + m[3:4]).astype(BF16)

    def cols(ref, j):
        return [ref[:, part * D_FF + j * FF_CHUNK:part * D_FF + (j + 1) * FF_CHUNK] for part in range(2)]

    def up(j, ubuf):
        ubuf[0:halo, :] = tail[j]
        for part, w in enumerate(cols(wu_ref, j)):
            ubuf[halo:halo + tm, part * FF_CHUNK:(part + 1) * FF_CHUNK] = _mm(h, w)
        tail[j] = ubuf[tm:tm + halo, :]

    def gate(j, ubuf):
        cw = jnp.concatenate(cols(cw_ref, j), axis=1)
        u = jnp.broadcast_to(jnp.concatenate(cols(cb_ref, j), axis=1), (tm, 2 * FF_CHUNK))
        for k in range(FFN_CONV):
            off = halo - (FFN_CONV - 1) + k
            u = u + cw[k:k + 1, :] * ubuf[off:off + tm, :]
        act[j] = (_silu(u[:, :FF_CHUNK]) * u[:, FF_CHUNK:]).astype(BF16)

    bufs = (ubuf0, ubuf1)
    up(0, ubuf0)
    for j in range(N_FF_CHUNKS):
        if j + 1 < N_FF_CHUNKS:
            up(j + 1, bufs[(j + 1) % 2])
        gate(j, bufs[j % 2])
    ffn = _mm(act[0], wd_ref[0])
    for j in range(1, N_FF_CHUNKS):
        ffn = ffn + _mm(act[j], wd_ref[j])
    out_ref[...] = _layer_norm_rows(DEEPNORM_ALPHA * x + m[5:6] * ffn, g_ref[...], b_ref[...])


def _ffn(x2, mod, w_up, conv_w, conv_b, w_down, ln_g, ln_b, nb, seq):
    tm = 512
    per_b = seq // tm
    rows = pl.BlockSpec((tm, D_MODEL), lambda b, i: (b * per_b + i, 0))
    return pl.pallas_call(
        _ffn_kernel,
        grid=(nb, per_b),
        in_specs=[rows, pl.BlockSpec((None, 6, D_MODEL), lambda b, i: (b, 0, 0)),
                  _const_spec(w_up.shape), _const_spec(conv_w.shape), _const_spec(conv_b.shape),
                  _const_spec(w_down.shape), _const_spec((1, D_MODEL)), _const_spec((1, D_MODEL))],
        out_specs=rows,
        out_shape=jax.ShapeDtypeStruct((nb * seq, D_MODEL), F32),
        scratch_shapes=[pltpu.VMEM((tm + FFN_HALO, 2 * FF_CHUNK), F32),
                        pltpu.VMEM((tm + FFN_HALO, 2 * FF_CHUNK), F32),
                        pltpu.VMEM((N_FF_CHUNKS, FFN_HALO, 2 * FF_CHUNK), F32),
                        pltpu.VMEM((N_FF_CHUNKS, tm, FF_CHUNK), BF16)],
        compiler_params=_cparams("parallel", "arbitrary"),
        name="conv_ffn_ln",
    )(x2, mod, w_up, conv_w, conv_b, w_down, ln_g.reshape(1, -1), ln_b.reshape(1, -1))


def _pack_in_proj(w_in):
    offs = np.concatenate([[0], np.cumsum(SECTION_SIZES)])
    sec = {n: w_in[:, offs[i]:offs[i + 1]] for i, n in enumerate(SECTION_NAMES)}
    w = BRANCH_WIDTH
    sec.update({f"q{g}": sec["dq"][:, g * w:(g + 1) * w] for g in range(DIL_GROUPS)}, k=sec["dk"], v=sec["dv"])
    sec["dt_pad"] = jnp.zeros((D_MODEL, LANES - SSD_HEADS), w_in.dtype)
    cat = lambda names: jnp.concatenate([sec[n] for n in names], axis=1).astype(BF16)
    return {"dil": cat(sorted(PD_COLS, key=PD_COLS.get)), "ssd": cat(("z", "xbc", "dt", "dt_pad")),
            "ret": cat(("rq", "rk", "rv", "rg")), "hgrn": cat(("hq", "hf", "hi", "hg"))}


def kernel(x, c, w_ada, b_ada, w_in, w_merge, ssd_conv_w, ssd_conv_b, ssd_dt_bias, ssd_a_log, ssd_d, ssd_norm_w, hgrn_lb, hgrn_norm_w, w_branch_out, w_o, ln1_g, ln1_b, w_up, ffn_conv_w, ffn_conv_b, w_down, ln2_g, ln2_b):
    nb, seq, _ = x.shape
    assert seq % (DIL_PATTERNS[-1][1] * DIL_QBLOCKS * DIL_BLOCK) == 0 and x.shape[-1] == D_MODEL
    mods = _modulation(c, w_ada, b_ada).reshape(DEPTH, nb, 6, D_MODEL)
    x2 = x.reshape(nb * seq, D_MODEL)
    for layer in range(DEPTH):
        mod = mods[layer]
        w_proj = _pack_in_proj(w_in[layer])
        x3 = x2.reshape(nb, seq, D_MODEL)
        o_dil = _dilated([a.reshape(nb, seq, -1) for a in _project_dilated(x2, mod, w_proj["dil"], seq)])
        o_ssd = _ssd(x3, mod, w_proj["ssd"], ssd_conv_w[layer], ssd_conv_b[layer], ssd_dt_bias[layer],
                     ssd_a_log[layer], ssd_d[layer], ssd_norm_w[layer])
        o_ret = _retention(x3, mod, w_proj["ret"])
        o_hgrn = _hgrn(x3, mod, w_proj["hgrn"], layer, hgrn_lb, hgrn_norm_w[layer])
        flat = lambda a: a.reshape(nb * seq, BRANCH_WIDTH)
        x2 = _merge(x2, mod, flat(o_ssd), o_dil, flat(o_ret), flat(o_hgrn), w_merge[layer].astype(BF16),
                    w_branch_out[layer].astype(BF16), w_o[layer].astype(BF16), ln1_g[layer], ln1_b[layer], seq)
        x2 = _ffn(x2, mod, w_up[layer].astype(BF16), ffn_conv_w[layer], ffn_conv_b[layer].reshape(1, -1),
                  w_down[layer].reshape(N_FF_CHUNKS, FF_CHUNK, D_MODEL).astype(BF16),
                  ln2_g[layer], ln2_b[layer], nb, seq)
    return x2.reshape(nb, seq, D_MODEL)
```

```python
import functools
import math

import numpy as np
import jax
import jax.numpy as jnp
from jax import lax
from jax.experimental import pallas as pl
from jax.experimental.pallas import tpu as pltpu

F32 = jnp.float32
BF16 = jnp.bfloat16

D_MODEL = 1024
DEPTH = 2
DEEPNORM_ALPHA = (2 * DEPTH) ** 0.25
NORM_EPS = 1e-5

BRANCH_WIDTH = D_MODEL // 2

SSD_HEAD_DIM = 64
SSD_HEADS = BRANCH_WIDTH // SSD_HEAD_DIM
SSD_GROUPS = 2
SSD_STATE = 64
SSD_CONV = 4
SSD_CONV_DIM = BRANCH_WIDTH + 2 * SSD_GROUPS * SSD_STATE

DIL_HEAD_DIM = 64
DIL_HEADS = BRANCH_WIDTH // DIL_HEAD_DIM
DIL_PATTERNS = ((128, 1), (512, 4), (2048, 16))
DIL_GROUPS = len(DIL_PATTERNS)
DIL_BLOCK = 128

RET_HEADS = 4
RET_DK = BRANCH_WIDTH // RET_HEADS

HGRN_HEADS = 4
HGRN_DK = BRANCH_WIDTH // HGRN_HEADS
HGRN_SUB = 4

D_FF = 256 * ((8 * D_MODEL // 3 + 255) // 256)
FFN_CONV = 3
FF_CHUNK = 256
N_FF_CHUNKS = D_FF // FF_CHUNK

SECTION_SIZES = (
    BRANCH_WIDTH, SSD_CONV_DIM, SSD_HEADS,
    DIL_GROUPS * BRANCH_WIDTH, BRANCH_WIDTH, BRANCH_WIDTH,
    BRANCH_WIDTH, BRANCH_WIDTH, BRANCH_WIDTH, BRANCH_WIDTH,
    BRANCH_WIDTH, BRANCH_WIDTH, BRANCH_WIDTH, BRANCH_WIDTH,
)
SECTION_NAMES = ("z", "xbc", "dt", "dq", "dk", "dv", "rq", "rk", "rv", "rg", "hq", "hf", "hi", "hg")

LANES = 128
SUBLANES = 8
V7X_VMEM_BYTES = 64 * 1024 * 1024
VMEM_LIMIT = V7X_VMEM_BYTES - 8 * 1024 * 1024
CHUNK = 128
PD_COLS = {n: i for i, n in enumerate(("q0", "q1", "k", "v", "q2"))}
MIXER_ROWS = 4


def _log2(n):
    return int(math.log2(n))


def _rows_per_step(nb):
    return MIXER_ROWS if nb % MIXER_ROWS == 0 else 1


def _cparams(*sem):
    return pltpu.CompilerParams(dimension_semantics=sem, vmem_limit_bytes=VMEM_LIMIT)


def _silu(v):
    return v * jax.nn.sigmoid(v)


def _mm(a, b):
    return jnp.dot(a, b, preferred_element_type=F32)


def _mm_nt(a, b):
    return lax.dot_general(a, b, (((1,), (1,)), ((), ())), preferred_element_type=F32)


def _mm_tn(a, b):
    return lax.dot_general(a, b, (((0,), (0,)), ((), ())), preferred_element_type=F32)


def _split2(a):
    hi = a.astype(BF16)
    lo = (a - hi.astype(F32)).astype(BF16)
    return hi, lo


def _split3(a):
    hi = a.astype(BF16)
    r = a - hi.astype(F32)
    mid = r.astype(BF16)
    lo = (r - mid.astype(F32)).astype(BF16)
    return hi, mid, lo


def _mm_f32(a, b):
    ah, al = _split2(a)
    bh, bl = _split2(b)
    return _mm(ah, bh) + (_mm(ah, bl) + _mm(al, bh))


def _cumsum_rows(tri, a, lo_rows=None):
    hi, mid, lo = _split3(a)
    main = _mm(jnp.concatenate([tri, tri], axis=1), jnp.concatenate([hi, mid], axis=0))
    if lo_rows is None:
        return main + _mm(tri, lo)
    return jnp.concatenate([main[:lo_rows] + _mm(tri[:lo_rows], lo), main[lo_rows:]], axis=0)


def _expand(a, e):
    hi, lo = _split2(a)
    return _mm(jnp.concatenate([hi, lo], axis=1), jnp.concatenate([e, e], axis=0))


def _block_diag(a):
    d = a.shape[1] // 2
    zero = jnp.zeros((a.shape[0], d), a.dtype)
    return jnp.concatenate([jnp.concatenate([a[:, :d], zero], axis=1),
                            jnp.concatenate([zero, a[:, d:]], axis=1)], axis=0)


def _iota(shape, dim):
    return lax.broadcasted_iota(jnp.int32, shape, dim)


def _tri(n):
    return jnp.where(_iota((n, n), 1) <= _iota((n, n), 0), 1.0, 0.0).astype(BF16)


def _head_expander(width):
    shape = (LANES, BRANCH_WIDTH)
    shift = _log2(width)
    return jnp.where(jnp.right_shift(_iota(shape, 1), shift) == _iota(shape, 0), 1.0, 0.0).astype(BF16)


def _layer_norm_rows(v, g, b):
    vc = v - jnp.mean(v, axis=-1, keepdims=True)
    return vc * lax.rsqrt(jnp.mean(vc * vc, axis=-1, keepdims=True) + NORM_EPS) * g + b


def _mod_kernel(c_ref, w_ref, b_ref, o_ref):
    o_ref[0] = _mm_f32(_silu(c_ref[...]), w_ref[0]) + b_ref[0]


def _modulation(c, w_ada, b_ada):
    nb = c.shape[0]
    tn = 1536
    return pl.pallas_call(
        _mod_kernel,
        grid=(DEPTH, 6 * D_MODEL // tn),
        in_specs=[pl.BlockSpec((nb, D_MODEL), lambda l, j: (0, 0)),
                  pl.BlockSpec((1, D_MODEL, tn), lambda l, j: (l, 0, j)),
                  pl.BlockSpec((1, 1, tn), lambda l, j: (l, 0, j))],
        out_specs=pl.BlockSpec((1, nb, tn), lambda l, j: (l, 0, j)),
        out_shape=jax.ShapeDtypeStruct((DEPTH, nb, 6 * D_MODEL), F32),
        compiler_params=_cparams("parallel", "parallel"),
        name="adaln_mod",
    )(c, w_ada, b_ada.reshape(DEPTH, 1, 6 * D_MODEL))


def _const_spec(shape):
    zeros = (0,) * len(shape)
    return pl.BlockSpec(shape, lambda *_: zeros, pipeline_mode=pl.Buffered(1))


def _destride_matrix(n, r):
    per = n // r
    i = _iota((n, n), 0)
    src = jnp.left_shift(jnp.bitwise_and(i, per - 1), _log2(r)) + jnp.right_shift(i, _log2(per))
    return jnp.where(_iota((n, n), 1) == src, 1.0, 0.0).astype(BF16)


def _proj_kernel(x_ref, mod_ref, wd_ref, od0_ref, od1_ref, od2_ref):
    m = mod_ref[...]
    h = (x_ref[...] * (1.0 + m[1:2]) + m[0:1]).astype(BF16)
    w = BRANCH_WIDTH
    col = _iota((1, 5 * w), 1)
    q_scale = jnp.where((col < 2 * w) | (col >= 4 * w), DIL_HEAD_DIM ** -0.5 / math.log(2.0), 1.0)
    d = (_mm(h, wd_ref[...]) * q_scale).astype(BF16)
    tm = d.shape[0]
    od0_ref[...] = jnp.concatenate([d[:, 0:w], d[:, 2 * w:4 * w]], axis=1)
    od1_ref[...] = _mm(_destride_matrix(tm, DIL_PATTERNS[1][1]), d[:, w:4 * w]).astype(BF16)
    od2_ref[...] = _mm(_destride_matrix(tm, DIL_PATTERNS[2][1]), d[:, 2 * w:5 * w]).astype(BF16)


PROJ_ROWS = 256
DIL_OPERAND_COLS = ((0, 1, 2), (0, 1, 2), (2, 0, 1))


def _project_dilated(x2, mod, wd, seq):
    t = x2.shape[0]
    tm = PROJ_ROWS
    per_b = seq // tm
    qkv = 3 * BRANCH_WIDTH
    return pl.pallas_call(
        _proj_kernel,
        grid=(t // tm,),
        in_specs=[pl.BlockSpec((tm, D_MODEL), lambda i: (i, 0)),
                  pl.BlockSpec((None, 6, D_MODEL), lambda i: (i // per_b, 0, 0)),
                  _const_spec(wd.shape)],
        out_specs=[pl.BlockSpec((tm, qkv), lambda i: (i, 0))] * DIL_GROUPS,
        out_shape=[jax.ShapeDtypeStruct((t, qkv), BF16)] * DIL_GROUPS,
        compiler_params=_cparams("parallel"),
        name="in_proj_dilated",
    )(x2, mod, wd)


SSD_HALO = SUBLANES


def _ssd_init(xbuf, st):
    xbuf[:, 0:SSD_HALO, :] = jnp.zeros((xbuf.shape[0], SSD_HALO, SSD_CONV_DIM), F32)
    st[...] = jnp.zeros_like(st)


def _ssd_chunk(p_ref, o_ref, half, cw_ref, cb_ref, dtb_ref, alog_ref, dsk_ref, nw_ref, xbuf, st):
    c = CHUNK
    halo = SSD_HALO
    xbc0 = BRANCH_WIDTH
    dt0 = BRANCH_WIDTH + SSD_CONV_DIM
    for b in range(p_ref.shape[0]):
        xbuf[b, halo:halo + c, :] = p_ref[b, :, xbc0:dt0]
        acc = jnp.broadcast_to(cb_ref[...], (c, SSD_CONV_DIM))
        for k in range(SSD_CONV):
            off = halo - (SSD_CONV - 1) + k
            acc = acc + cw_ref[k:k + 1, :] * xbuf[b, off:off + c, :]
        xbuf[b, 0:halo, :] = xbuf[b, c:c + halo, :]
        y = _silu(acc)
        xs = y[:, :BRANCH_WIDTH]
        bm = y[:, BRANCH_WIDTH:BRANCH_WIDTH + LANES]
        cm = y[:, BRANCH_WIDTH + LANES:]

        dtr = p_ref[b, :, dt0:dt0 + LANES] + dtb_ref[...]
        dt = jnp.maximum(dtr, 0.0) + jnp.log1p(jnp.exp(-jnp.abs(dtr)))
        da = dt * (-jnp.exp(alog_ref[...]))
        cs = _cumsum_rows(_tri(c), da)
        cs_t = cs.T
        tot = cs[c - 1:c, :]

        stack = jnp.concatenate([dt, jnp.exp(tot - cs), jnp.exp(cs), jnp.broadcast_to(jnp.exp(tot), (8, LANES))], axis=0)
        ex = _expand(stack, _head_expander(SSD_HEAD_DIM))
        dt_e, ds_e, ecs_e, dec_e = ex[0:c], ex[c:2 * c], ex[2 * c:3 * c], ex[3 * c:3 * c + 1]

        xdt = xs * dt_e
        xds = xdt * ds_e
        causal = _iota((c, c), 1) <= _iota((c, c), 0)
        lane = _iota((c, LANES), 1)
        bm16 = bm.astype(BF16)
        cbs = []
        for g in range(SSD_GROUPS):
            cm_g = jnp.where(jnp.right_shift(lane, _log2(SSD_STATE)) == g, cm, 0.0).astype(BF16)
            cbs.append(_mm_nt(cm_g, bm16))
        parts = []
        for p in range(SSD_HEADS // 2):
            g = (2 * p) // (SSD_HEADS // SSD_GROUPS)
            ms = []
            for e in range(2):
                h = 2 * p + e
                diff = cs[:, h:h + 1] - cs_t[h:h + 1, :]
                ms.append((cbs[g] * jnp.exp(jnp.where(causal, diff, -jnp.inf))).astype(BF16))
            xp = xdt[:, p * LANES:(p + 1) * LANES]
            xbd = jnp.concatenate([jnp.where(lane < SSD_HEAD_DIM, xp, 0.0), jnp.where(lane >= SSD_HEAD_DIM, xp, 0.0)],
                                  axis=0).astype(BF16)
            parts.append(_mm(jnp.concatenate(ms, axis=1), xbd))
        y_diag = jnp.concatenate(parts, axis=1)

        s_prev = st[b]
        y_off = _mm(cm.astype(BF16), s_prev.astype(BF16)) * ecs_e
        upd = _mm(bm.T.astype(BF16), xds.astype(BF16))
        shape = (LANES, BRANCH_WIDTH)
        same_group = (jnp.right_shift(_iota(shape, 0), _log2(SSD_STATE))
                      == jnp.right_shift(_iota(shape, 1), _log2(BRANCH_WIDTH // SSD_GROUPS)))
        st[b] = s_prev * dec_e + jnp.where(same_group, upd, 0.0)

        yv = (y_diag + y_off + xs * dsk_ref[...]) * _silu(p_ref[b, :, 0:BRANCH_WIDTH])
        gw = BRANCH_WIDTH // SSD_GROUPS
        outs = []
        for g in range(SSD_GROUPS):
            yg = yv[:, g * gw:(g + 1) * gw]
            outs.append(yg * lax.rsqrt(jnp.mean(yg * yg, axis=-1, keepdims=True) + NORM_EPS))
        o_ref[b, half * c:(half + 1) * c, :] = (jnp.concatenate(outs, axis=1) * nw_ref[...]).astype(o_ref.dtype)


def _ssd(x, mod, w, conv_w, conv_b, dt_bias, a_log, d_skip, norm_w):
    pad = LANES - SSD_HEADS
    row = lambda v: v.reshape(1, -1).astype(F32)
    aux = (conv_w.astype(F32), row(conv_b), row(jnp.pad(dt_bias, (0, pad))), row(jnp.pad(a_log, (0, pad))),
           row(jnp.repeat(d_skip, SSD_HEAD_DIM)), row(norm_w))
    return _fused_mixer("ssd_mixer", _ssd_init, _ssd_chunk, x, mod, w, aux,
                        [(CHUNK + SSD_HALO, SSD_CONV_DIM), (LANES, BRANCH_WIDTH)])


def _alibi_slopes(n):
    def pow2(k):
        start = 2.0 ** (-8.0 / k)
        return [start ** (i + 1) for i in range(k)]
    if math.log2(n).is_integer():
        s = pow2(n)
    else:
        c = 2 ** math.floor(math.log2(n))
        s = pow2(c) + pow2(2 * c)[0::2][: n - c]
    return [float(np.float32(v)) for v in s]


def _dil_kernel(dilation, q_ref, kc_ref, kp_ref, vc_ref, vp_ref, bias_ref, o_ref, lse_ref):
    blk = DIL_BLOCK
    rho = pl.program_id(2)

    def rows_of(ref, start, sl):
        if len(ref.shape) == 2:
            return ref[start:start + blk, sl]
        per = ref.shape[1]
        return jnp.concatenate([ref[t, :, sl] for t in range(start // per, (start + blk) // per)], axis=0)

    lane = _iota((blk, LANES), 1)
    for sb in range(DIL_QBLOCKS):
        table = jnp.where(pl.program_id(1) > 0, 1, 0) if sb == 0 else 1
        rows = pl.ds(rho + sb * blk * dilation, blk, stride=dilation) if dilation > 1 else pl.ds(sb * blk, blk)
        lse_tile = jnp.zeros((blk, LANES), F32)
        for p in range(DIL_HEADS // 2):
            sl = slice(p * LANES, (p + 1) * LANES)
            qp = rows_of(q_ref, sb * blk, sl).astype(F32)
            if sb == 0:
                kk = jnp.concatenate([rows_of(kp_ref, 0, sl), rows_of(kc_ref, 0, sl)], axis=0)
                vv = jnp.concatenate([rows_of(vp_ref, 0, sl), rows_of(vc_ref, 0, sl)], axis=0)
            else:
                kk = jnp.concatenate([rows_of(kc_ref, (sb - 1) * blk, sl), rows_of(kc_ref, sb * blk, sl)], axis=0)
                vv = jnp.concatenate([rows_of(vc_ref, (sb - 1) * blk, sl), rows_of(vc_ref, sb * blk, sl)], axis=0)
            pair = []
            for e in range(2):
                h = 2 * p + e
                own = (lane < DIL_HEAD_DIM) if e == 0 else (lane >= DIL_HEAD_DIM)
                qm = jnp.where(own, qp, 0.0).astype(BF16)
                s = _mm_nt(qm, kk) + bias_ref[table, h]
                m = jnp.max(s, axis=-1, keepdims=True)
                pexp = jnp.exp2(s - m)
                l = jnp.sum(pexp, axis=-1, keepdims=True)
                pair.append(_mm(pexp.astype(BF16), vv) / l)
                lse_tile = jnp.where(lane == h, (m + jnp.log2(l)) * math.log(2.0), lse_tile)
            o_ref[p, rows, :] = jnp.where(lane < DIL_HEAD_DIM, pair[0], pair[1])
        lse_ref[rows, :] = lse_tile


DIL_QBLOCKS = 2


def _dil_bias_tables(g):
    window, r = DIL_PATTERNS[g]
    blk = DIL_BLOCK
    dist = np.arange(blk)[:, None] - np.arange(2 * blk)[None, :] + blk
    in_window = (dist >= 0) & (dist <= window // r)
    current = np.arange(2 * blk)[None, :] >= blk
    slopes = np.asarray(_alibi_slopes(DIL_GROUPS * DIL_HEADS)[g * DIL_HEADS:(g + 1) * DIL_HEADS])
    penalty = -(dist * r) / math.log(2.0)
    tables = [slopes[:, None, None] * np.where(valid, penalty, -np.inf)[None]
              for valid in (in_window & current, in_window)]
    return jnp.asarray(np.stack(tables), F32)


def _dilated_group(pd, g):
    nb, seq, _ = pd.shape
    window, r = DIL_PATTERNS[g]
    n = seq // r
    qrows = DIL_QBLOCKS * DIL_BLOCK
    w = BRANCH_WIDTH
    qc, kc, vc = DIL_OPERAND_COLS[g]
    if r == 1:
        view = pd
        cur = lambda col: pl.BlockSpec((None, qrows, w), lambda b, i, rho: (b, i, col))
        prev = lambda col: pl.BlockSpec((None, DIL_BLOCK, w),
                                        lambda b, i, rho: (b, jnp.maximum(DIL_QBLOCKS * i - 1, 0), col))
    else:
        per = PROJ_ROWS // r
        view = pd.reshape(nb, seq // PROJ_ROWS, r, per, 3 * w)
        cur = lambda col: pl.BlockSpec((None, qrows // per, None, per, w), lambda b, i, rho: (b, i, rho, 0, col))
        prev = lambda col: pl.BlockSpec((None, DIL_BLOCK // per, None, per, w),
                                        lambda b, i, rho: (b, jnp.maximum(DIL_QBLOCKS * i - 1, 0), rho, 0, col))
    bias = _dil_bias_tables(g)
    pairs = DIL_HEADS // 2
    return pl.pallas_call(
        functools.partial(_dil_kernel, r),
        grid=(nb, n // qrows, r),
        in_specs=[cur(qc), cur(kc), prev(kc), cur(vc), prev(vc), _const_spec(bias.shape)],
        out_specs=[pl.BlockSpec((pairs, None, qrows * r, LANES), lambda b, i, rho: (0, b, i, 0)),
                   pl.BlockSpec((None, qrows * r, LANES), lambda b, i, rho: (b, i, 0))],
        out_shape=[jax.ShapeDtypeStruct((pairs, nb, seq, LANES), F32),
                   jax.ShapeDtypeStruct((nb, seq, LANES), F32)],
        compiler_params=_cparams("parallel", "parallel", "arbitrary"),
        name=f"dilated_attn_g{g}",
    )(view, view, view, view, view, bias)


def _combine_dilated(lses, outs):
    m = functools.reduce(jnp.maximum, lses)
    es = [jnp.exp(l - m) for l in lses]
    den = functools.reduce(lambda a, b: a + b, es)
    e = _head_expander(DIL_HEAD_DIM)
    acc = None
    for ev, o in zip(es, outs):
        term = _expand(ev / den, e) * jnp.concatenate(o, axis=1)
        acc = term if acc is None else acc + term
    return acc


def _dilated(pds):
    nb, seq, _ = pds[0].shape
    res = [_dilated_group(pd, g) for g, pd in enumerate(pds)]
    t = nb * seq
    return [r[1].reshape(t, LANES) for r in res], [r[0].reshape(DIL_HEADS // 2, t, LANES) for r in res]


def _fused_mixer_kernel(init_fn, chunk_fn, n_aux, x0_ref, xa_ref, xb_ref, mod_ref, modn_ref, w_ref, *rest):
    aux, o_ref, p0, p1, scratch = rest[:n_aux], rest[n_aux], rest[n_aux + 1], rest[n_aux + 2], rest[n_aux + 3:]
    rows = x0_ref.shape[0]
    m = mod_ref[...]
    wrap = pl.program_id(1) == pl.num_programs(1) - 1

    def project(x_ref, p_ref, mv):
        h = jnp.concatenate([(x_ref[b] * (1.0 + mv[b, 1:2]) + mv[b, 0:1]).astype(BF16) for b in range(rows)], axis=0)
        res = _mm(h, w_ref[...])
        for b in range(rows):
            p_ref[b] = res[b * CHUNK:(b + 1) * CHUNK]

    @pl.when(pl.program_id(1) == 0)
    def _():
        init_fn(*scratch)

    @pl.when((pl.program_id(0) == 0) & (pl.program_id(1) == 0))
    def _():
        project(x0_ref, p0, m)

    project(xa_ref, p1, m)
    chunk_fn(p0, o_ref, 0, *aux, *scratch)
    project(xb_ref, p0, jnp.where(wrap, modn_ref[...], m))
    chunk_fn(p1, o_ref, 1, *aux, *scratch)


def _fused_mixer(name, init_fn, chunk_fn, x, mod, w, aux, scratch_shapes):
    nb, seq, _ = x.shape
    rows = _rows_per_step(nb)
    nc = seq // CHUNK
    n = w.shape[1]
    steps = nc // 2
    last_b = nb // rows - 1
    xblock = (rows, CHUNK, D_MODEL)

    def next_chunk(b, s):
        wrap = s == steps - 1
        return jnp.where(wrap, jnp.minimum(b + 1, last_b), b), jnp.where(wrap, 0, 2 * s + 2), 0

    return pl.pallas_call(
        functools.partial(_fused_mixer_kernel, init_fn, chunk_fn, len(aux)),
        grid=(nb // rows, steps),
        in_specs=[pl.BlockSpec(xblock, lambda b, s: (0, 0, 0)), pl.BlockSpec(xblock, lambda b, s: (b, 2 * s + 1, 0)),
                  pl.BlockSpec(xblock, next_chunk),
                  pl.BlockSpec((rows, 6, D_MODEL), lambda b, s: (b, 0, 0)),
                  pl.BlockSpec((rows, 6, D_MODEL), lambda b, s: (jnp.minimum(b + 1, last_b), 0, 0)),
                  _const_spec(w.shape)]
        + [_const_spec(a.shape) for a in aux],
        out_specs=pl.BlockSpec((rows, 2 * CHUNK, BRANCH_WIDTH), lambda b, s: (b, s, 0)),
        out_shape=jax.ShapeDtypeStruct((nb, seq, BRANCH_WIDTH), BF16),
        scratch_shapes=[pltpu.VMEM((rows, CHUNK, n), F32), pltpu.VMEM((rows, CHUNK, n), F32)]
        + [pltpu.VMEM((rows,) + s, F32) for s in scratch_shapes],
        compiler_params=_cparams("arbitrary", "arbitrary"),
        name=name,
    )(x, x, x, mod, mod, w, *aux)


def _ret_init(st):
    st[...] = jnp.zeros_like(st)


def _ret_chunk(p_ref, o_ref, half, st):
    c = CHUNK
    w = BRANCH_WIDTH
    dk = RET_DK
    scale = dk ** -0.5
    row = _iota((c, c), 0)
    col = _iota((c, c), 1)
    rel = (row - col).astype(F32)
    pos = row.astype(F32)
    lgs = [math.log(1.0 - 2.0 ** (-5.0 - h)) for h in range(RET_HEADS)]

    def pair_of(f):
        return [jnp.concatenate([f(lgs[2 * pr]), f(lgs[2 * pr + 1])], axis=1) for pr in range(RET_HEADS // 2)]

    decay = pair_of(lambda lg: jnp.where(row >= col, jnp.exp(lg * jnp.maximum(rel, 0.0)), 0.0) * scale)
    q_scale = pair_of(lambda lg: jnp.exp(lg * (pos + 1.0)))
    k_scale = pair_of(lambda lg: jnp.exp(lg * (c - 1.0 - pos)) * scale)
    on_diag = (jnp.right_shift(_iota((2 * dk, 2 * dk), 0), _log2(dk))
               == jnp.right_shift(_iota((2 * dk, 2 * dk), 1), _log2(dk)))
    upper = _iota((2 * dk, 2 * dk), 0) < dk
    for b in range(p_ref.shape[0]):
        outs = []
        for pr in range(RET_HEADS // 2):
            sl = lambda i: slice(i * w + pr * 2 * dk, i * w + (pr + 1) * 2 * dk)
            qf, kf = p_ref[b, :, sl(0)], p_ref[b, :, sl(1)]
            v16 = p_ref[b, :, sl(2)].astype(BF16)
            scores = _mm_nt(qf.astype(BF16), _block_diag(kf.astype(BF16))) * decay[pr]
            inner = _mm(scores.astype(BF16), _block_diag(v16))
            s_prev = st[b, pr]
            o = inner + _mm((qf * q_scale[pr]).astype(BF16), s_prev.astype(BF16))
            upd = _mm_tn((kf * k_scale[pr]).astype(BF16), v16)
            carry = jnp.where(upper, math.exp(lgs[2 * pr] * c), math.exp(lgs[2 * pr + 1] * c))
            st[b, pr] = s_prev * carry + jnp.where(on_diag, upd, 0.0)
            for e in range(2):
                oh = o[:, e * dk:(e + 1) * dk]
                oc = oh - jnp.mean(oh, axis=-1, keepdims=True)
                outs.append(oc * lax.rsqrt(jnp.mean(oc * oc, axis=-1, keepdims=True) + NORM_EPS))
        gate = _silu(p_ref[b, :, 3 * w:4 * w])
        o_ref[b, half * c:(half + 1) * c, :] = (jnp.concatenate(outs, axis=1) * gate).astype(o_ref.dtype)


def _retention(x, mod, w):
    return _fused_mixer("retention", _ret_init, _ret_chunk, x, mod, w, (),
                        [(RET_HEADS // 2, 2 * RET_DK, 2 * RET_DK)])


def _hgrn_init(st):
    st[...] = jnp.zeros_like(st)


def _hgrn_chunk(layer, p_ref, o_ref, half, lb_ref, nw_ref, st):
    c = CHUNK
    width = BRANCH_WIDTH
    neg_inf = -jnp.inf

    rows = [lb_ref[l:l + 1, :] for l in range(DEPTH)]
    mx = functools.reduce(jnp.maximum, rows)
    es = [jnp.exp(r - mx) for r in rows]
    den = functools.reduce(lambda a, b: a + b, es)
    sm = [e / den for e in es]
    lb = functools.reduce(lambda a, b: a + b, sm[:layer + 1]) - sm[0]

    ti = _iota((c, c), 0)
    tj = _iota((c, c), 1)
    le_t = jnp.where(tj <= ti, 1.0, 0.0)
    sizes = [c >> k for k in range(_log2(c // HGRN_SUB))]
    mats = [le_t]
    for size in sizes:
        sh = _log2(size)
        last_lower = jnp.left_shift(jnp.right_shift(ti, sh), sh) + (size // 2 - 1)
        mats.append(le_t - jnp.where(tj <= last_lower, 1.0, 0.0))
    seg = jnp.concatenate(mats, axis=0).astype(BF16)
    rowi = _iota((c, width), 0)
    sub = jnp.bitwise_and(rowi, HGRN_SUB - 1)
    heads = [slice(h * HGRN_DK, (h + 1) * HGRN_DK) for h in range(HGRN_HEADS)]

    nrows = range(p_ref.shape[0])
    pre = []
    for b in nrows:
        forget = lb + (1.0 - lb) * jax.nn.sigmoid(p_ref[b, :, width:2 * width])
        lf = jnp.log(forget)
        kk = 1.0 - forget
        q = _silu(p_ref[b, :, 0:width])
        vf = p_ref[b, :, 2 * width:3 * width]
        pre.append((kk, q, vf, vf.astype(BF16), _cumsum_rows(seg, lf, lo_rows=c)))

    exact = []
    for kk, q, vf, v16, sums in pre:
        lam = sums[0:c]
        o_acc = [jnp.zeros((c, HGRN_DK), F32) for _ in heads]
        for d in range(HGRN_SUB):
            if d == 0:
                prod = q * kk
                vd = vf
            else:
                ld = pltpu.roll(lam, d, 0)
                prod = q * pltpu.roll(kk, d, 0) * jnp.exp(jnp.where(sub >= d, lam - ld, neg_inf))
                vd = pltpu.roll(vf, d, 0)
            for h, sl in enumerate(heads):
                o_acc[h] = o_acc[h] + jnp.sum(prod[:, sl], axis=-1, keepdims=True) * vd[:, sl]
        exact.append(o_acc)

    blocks = []
    for kk, q, vf, v16, sums in pre:
        attn = [jnp.zeros((c, c), F32) for _ in heads]
        for k, size in enumerate(sizes):
            mid = size // 2
            rel = sums[(k + 1) * c:(k + 2) * c]
            upper = jnp.bitwise_and(rowi, size - 1) >= mid
            z = (jnp.where(upper, q, kk) * jnp.exp(-jnp.abs(rel))).astype(BF16)
            sh = _log2(size)
            pairs = ((jnp.right_shift(ti, sh) == jnp.right_shift(tj, sh))
                     & (jnp.bitwise_and(ti, size - 1) >= mid) & (jnp.bitwise_and(tj, size - 1) < mid))
            for h, sl in enumerate(heads):
                attn[h] = attn[h] + jnp.where(pairs, _mm_nt(z[:, sl], z[:, sl]), 0.0)
        blocks.append(attn)

    for b in nrows:
        kk, q, vf, v16, sums = pre[b]
        lam = sums[0:c]
        lam_last = lam[c - 1:c, :]
        q_in = (q * jnp.exp(lam)).astype(BF16)
        k_out = (kk * jnp.exp(lam_last - lam)).astype(BF16)
        e_last = jnp.exp(lam_last)
        outs = []
        for h, sl in enumerate(heads):
            s_prev = st[b, h]
            o = (exact[b][h] + _mm(blocks[b][h].astype(BF16), v16[:, sl])
                 + _mm_nt(q_in[:, sl], s_prev.astype(BF16)))
            st[b, h] = s_prev * e_last[:, sl] + _mm_tn(v16[:, sl], k_out[:, sl])
            outs.append(o * lax.rsqrt(jnp.mean(o * o, axis=-1, keepdims=True) + NORM_EPS) * nw_ref[...])
        gate = _silu(p_ref[b, :, 3 * width:4 * width])
        o_ref[b, half * c:(half + 1) * c, :] = (jnp.concatenate(outs, axis=1) * gate).astype(o_ref.dtype)


def _hgrn(x, mod, w, layer, hgrn_lb, norm_w):
    aux = (hgrn_lb.astype(F32), norm_w.reshape(1, HGRN_DK).astype(F32))
    return _fused_mixer("hgrn2", _hgrn_init, functools.partial(_hgrn_chunk, layer), x, mod, w, aux,
                        [(HGRN_HEADS, HGRN_DK, HGRN_DK)])


def _merge_kernel(x_ref, mod_ref, ssd_ref, l0_ref, l1_ref, l2_ref, d0_ref, d1_ref, d2_ref, ret_ref, hgrn_ref,
                  wm_ref, wb_ref, wo_ref, g_ref, b_ref, out_ref):
    m = mod_ref[...]
    x = x_ref[...]
    h = (x * (1.0 + m[1:2]) + m[0:1]).astype(BF16)
    pairs = range(DIL_HEADS // 2)
    dil = _combine_dilated([r[...] for r in (l0_ref, l1_ref, l2_ref)],
                           [[r[p] for p in pairs] for r in (d0_ref, d1_ref, d2_ref)]).astype(BF16)
    y = None
    for i, o in enumerate((ssd_ref[...], dil, ret_ref[...], hgrn_ref[...])):
        gate = jax.nn.sigmoid(_mm(h, wm_ref[:, i * D_MODEL:(i + 1) * D_MODEL]))
        term = gate * _mm(o, wb_ref[i])
        y = term if y is None else y + term
    mix = _mm(y.astype(BF16), wo_ref[...])
    out_ref[...] = _layer_norm_rows(DEEPNORM_ALPHA * x + m[2:3] * mix, g_ref[...], b_ref[...])


def _merge(x2, mod, o_ssd, dil, o_ret, o_hgrn, w_merge, w_branch, w_o, ln_g, ln_b, seq):
    t = x2.shape[0]
    tm = 512
    per_b = seq // tm
    lses, outs = dil
    rows = pl.BlockSpec((tm, D_MODEL), lambda i: (i, 0))
    brow = pl.BlockSpec((tm, BRANCH_WIDTH), lambda i: (i, 0))
    lrow = pl.BlockSpec((tm, LANES), lambda i: (i, 0))
    drow = pl.BlockSpec((DIL_HEADS // 2, tm, LANES), lambda i: (0, i, 0))
    return pl.pallas_call(
        _merge_kernel,
        grid=(t // tm,),
        in_specs=[rows, pl.BlockSpec((None, 6, D_MODEL), lambda i: (i // per_b, 0, 0)), brow]
        + [lrow] * DIL_GROUPS + [drow] * DIL_GROUPS + [brow, brow]
        + [_const_spec(w_merge.shape), _const_spec(w_branch.shape), _const_spec(w_o.shape),
           _const_spec((1, D_MODEL)), _const_spec((1, D_MODEL))],
        out_specs=rows,
        out_shape=jax.ShapeDtypeStruct((t, D_MODEL), F32),
        compiler_params=_cparams("parallel"),
        name="merge_out_ln",
    )(x2, mod, o_ssd, *lses, *outs, o_ret, o_hgrn, w_merge, w_branch, w_o, ln_g.reshape(1, -1), ln_b.reshape(1, -1))


FFN_HALO = SUBLANES


def _ffn_kernel(x_ref, mod_ref, wu_ref, cw_ref, cb_ref, wd_ref, g_ref, b_ref, out_ref, ubuf0, ubuf1, tail, act):
    tm = x_ref.shape[0]
    halo = FFN_HALO

    @pl.when(pl.program_id(1) == 0)
    def _():
        tail[...] = jnp.zeros_like(tail)

    m = mod_ref[...]
    x = x_ref[...]
    h = (x * (1.0 + m[4:5]) + m[3:4]).astype(BF16)

    def cols(ref, j):
        return [ref[:, part * D_FF + j * FF_CHUNK:part * D_FF + (j + 1) * FF_CHUNK] for part in range(2)]

    def up(j, ubuf):
        ubuf[0:halo, :] = tail[j]
        for part, w in enumerate(cols(wu_ref, j)):
            ubuf[halo:halo + tm, part * FF_CHUNK:(part + 1) * FF_CHUNK] = _mm(h, w)
        tail[j] = ubuf[tm:tm + halo, :]

    def gate(j, ubuf):
        cw = jnp.concatenate(cols(cw_ref, j), axis=1)
        u = jnp.broadcast_to(jnp.concatenate(cols(cb_ref, j), axis=1), (tm, 2 * FF_CHUNK))
        for k in range(FFN_CONV):
            off = halo - (FFN_CONV - 1) + k
            u = u + cw[k:k + 1, :] * ubuf[off:off + tm, :]
        act[j] = (_silu(u[:, :FF_CHUNK]) * u[:, FF_CHUNK:]).astype(BF16)

    bufs = (ubuf0, ubuf1)
    up(0, ubuf0)
    for j in range(N_FF_CHUNKS):
        if j + 1 < N_FF_CHUNKS:
            up(j + 1, bufs[(j + 1) % 2])
        gate(j, bufs[j % 2])
    ffn = _mm(act[0], wd_ref[0])
    for j in range(1, N_FF_CHUNKS):
        ffn = ffn + _mm(act[j], wd_ref[j])
    out_ref[...] = _layer_norm_rows(DEEPNORM_ALPHA * x + m[5:6] * ffn, g_ref[...], b_ref[...])


def _ffn(x2, mod, w_up, conv_w, conv_b, w_down, ln_g, ln_b, nb, seq):
    tm = 512
    per_b = seq // tm
    rows = pl.BlockSpec((tm, D_MODEL), lambda b, i: (b * per_b + i, 0))
    return pl.pallas_call(
        _ffn_kernel,
        grid=(nb, per_b),
        in_specs=[rows, pl.BlockSpec((None, 6, D_MODEL), lambda b, i: (b, 0, 0)),
                  _const_spec(w_up.shape), _const_spec(conv_w.shape), _const_spec(conv_b.shape),
                  _const_spec(w_down.shape), _const_spec((1, D_MODEL)), _const_spec((1, D_MODEL))],
        out_specs=rows,
        out_shape=jax.ShapeDtypeStruct((nb * seq, D_MODEL), F32),
        scratch_shapes=[pltpu.VMEM((tm + FFN_HALO, 2 * FF_CHUNK), F32),
                        pltpu.VMEM((tm + FFN_HALO, 2 * FF_CHUNK), F32),
                        pltpu.VMEM((N_FF_CHUNKS, FFN_HALO, 2 * FF_CHUNK), F32),
                        pltpu.VMEM((N_FF_CHUNKS, tm, FF_CHUNK), BF16)],
        compiler_params=_cparams("parallel", "arbitrary"),
        name="conv_ffn_ln",
    )(x2, mod, w_up, conv_w, conv_b, w_down, ln_g.reshape(1, -1), ln_b.reshape(1, -1))


def _pack_in_proj(w_in):
    offs = np.concatenate([[0], np.cumsum(SECTION_SIZES)])
    sec = {n: w_in[:, offs[i]:offs[i + 1]] for i, n in enumerate(SECTION_NAMES)}
    w = BRANCH_WIDTH
    sec.update({f"q{g}": sec["dq"][:, g * w:(g + 1) * w] for g in range(DIL_GROUPS)}, k=sec["dk"], v=sec["dv"])
    sec["dt_pad"] = jnp.zeros((D_MODEL, LANES - SSD_HEADS), w_in.dtype)
    cat = lambda names: jnp.concatenate([sec[n] for n in names], axis=1).astype(BF16)
    return {"dil": cat(sorted(PD_COLS, key=PD_COLS.get)), "ssd": cat(("z", "xbc", "dt", "dt_pad")),
            "ret": cat(("rq", "rk", "rv", "rg")), "hgrn": cat(("hq", "hf", "hi", "hg"))}


def kernel(x, c, w_ada, b_ada, w_in, w_merge, ssd_conv_w, ssd_conv_b, ssd_dt_bias, ssd_a_log, ssd_d, ssd_norm_w, hgrn_lb, hgrn_norm_w, w_branch_out, w_o, ln1_g, ln1_b, w_up, ffn_conv_w, ffn_conv_b, w_down, ln2_g, ln2_b):
    nb, seq, _ = x.shape
    assert seq % (DIL_PATTERNS[-1][1] * DIL_QBLOCKS * DIL_BLOCK) == 0 and x.shape[-1] == D_MODEL
    mods = _modulation(c, w_ada, b_ada).reshape(DEPTH, nb, 6, D_MODEL)
    x2 = x.reshape(nb * seq, D_MODEL)
    for layer in range(DEPTH):
        mod = mods[layer]
        w_proj = _pack_in_proj(w_in[layer])
        x3 = x2.reshape(nb, seq, D_MODEL)
        o_dil = _dilated([a.reshape(nb, seq, -1) for a in _project_dilated(x2, mod, w_proj["dil"], seq)])
        o_ssd = _ssd(x3, mod, w_proj["ssd"], ssd_conv_w[layer], ssd_conv_b[layer], ssd_dt_bias[layer],
                     ssd_a_log[layer], ssd_d[layer], ssd_norm_w[layer])
        o_ret = _retention(x3, mod, w_proj["ret"])
        o_hgrn = _hgrn(x3, mod, w_proj["hgrn"], layer, hgrn_lb, hgrn_norm_w[layer])
        flat = lambda a: a.reshape(nb * seq, BRANCH_WIDTH)
        x2 = _merge(x2, mod, flat(o_ssd), o_dil, flat(o_ret), flat(o_hgrn), w_merge[layer].astype(BF16),
                    w_branch_out[layer].astype(BF16), w_o[layer].astype(BF16), ln1_g[layer], ln1_b[layer], seq)
        x2 = _ffn(x2, mod, w_up[layer].astype(BF16), ffn_conv_w[layer], ffn_conv_b[layer].reshape(1, -1),
                  w_down[layer].reshape(N_FF_CHUNKS, FF_CHUNK, D_MODEL).astype(BF16),
                  ln2_g[layer], ln2_b[layer], nb, seq)
    return x2.reshape(nb, seq, D_MODEL)
```
